```python
import jax, jax.numpy as jnp
from jax import lax
import numpy as np

D_MODEL = 1024
BATCH = 4
SEQ = 8192
DEPTH = 1
DEC_BATCH = 128
DEC_SEQ = 8
PAST_LEN = 8192
PAGE_SIZE = 128

C_CONV = D_MODEL // 2
HEAD_DIM = 64
N_HEADS = D_MODEL // (2 * HEAD_DIM)
N_KV_HEADS = N_HEADS // 2
GROUP = N_HEADS // N_KV_HEADS
N_IDX_HEADS = 16
IDX_DIM = 64
TOPK_MAX = 256
Q_BLOCK = 128
CONV_WIDTH = 31
ROPE_THETA = 500000.0
ROPE_FRACTION = 4
D_MIX = C_CONV + N_HEADS * HEAD_DIM
D_FF = -(-(8 * D_MODEL) // (3 * 256)) * 256
IN_SPLIT_SIZES = (C_CONV, C_CONV, N_HEADS * HEAD_DIM, N_KV_HEADS * HEAD_DIM,
                  N_KV_HEADS * HEAD_DIM, N_IDX_HEADS * IDX_DIM, IDX_DIM)
D_IN = sum(IN_SPLIT_SIZES) + N_IDX_HEADS

kernel_name = 'hymba_conformer_dsa_decoder_step'


def rms_norm(x, g, eps=1e-6):
    xf = x.astype(jnp.float32)
    y = xf * lax.rsqrt(jnp.mean(xf * xf, axis=-1, keepdims=True) + eps)
    return (y * g.astype(jnp.float32)).astype(x.dtype)


def layer_norm(x, g, b, eps=1e-5):
    xf = x.astype(jnp.float32)
    mu = jnp.mean(xf, axis=-1, keepdims=True)
    xc = xf - mu
    y = xc * lax.rsqrt(jnp.mean(xc * xc, axis=-1, keepdims=True) + eps)
    return (y * g.astype(jnp.float32) + b.astype(jnp.float32)).astype(x.dtype)


def rope_partial(x, pos):
    rd = x.shape[-1] // ROPE_FRACTION
    half = rd // 2
    inv_freq = ROPE_THETA ** (-jnp.arange(half, dtype=jnp.float32) / half)
    ang = pos.astype(jnp.float32)[:, None] * inv_freq[None, :]
    cos = jnp.cos(ang)[:, None, :]
    sin = jnp.sin(ang)[:, None, :]
    xr = x[..., :rd].astype(jnp.float32)
    x1, x2 = xr[..., :half], xr[..., half:]
    rot = jnp.concatenate([x1 * cos - x2 * sin, x2 * cos + x1 * sin], axis=-1)
    return jnp.concatenate([rot.astype(x.dtype), x[..., rd:]], axis=-1)


def mixer_inputs(x, pos, norm_g, w_in, q_norm_g, k_norm_g):
    B, T, _ = x.shape
    h = rms_norm(x, norm_g)
    z = jnp.einsum('btd,de->bte', h, w_in)
    offs = np.cumsum(IN_SPLIT_SIZES).tolist()
    u_val, u_gate, q, k, v, qi, ki, wi = jnp.split(z, offs, axis=-1)
    u = u_val * jax.nn.sigmoid(u_gate)
    q = rope_partial(rms_norm(q.reshape(B, T, N_HEADS, HEAD_DIM), q_norm_g), pos)
    k = rope_partial(rms_norm(k.reshape(B, T, N_KV_HEADS, HEAD_DIM), k_norm_g), pos)
    v = v.reshape(B, T, N_KV_HEADS, HEAD_DIM)
    qi = rope_partial(qi.reshape(B, T, N_IDX_HEADS, IDX_DIM), pos)
    ki = rope_partial(ki[:, :, None, :], pos)[:, :, 0, :]
    wi = wi * ((N_IDX_HEADS * IDX_DIM) ** -0.5)
    return u, q, k, v, qi, ki, wi


def conv_mixer(u, past_u, dw_w, dw_b, ln_g, ln_b):
    full = jnp.concatenate([past_u, u], axis=1)
    y = lax.conv_general_dilated(full, dw_w[:, None, :], window_strides=(1,), padding='VALID',
                                 dimension_numbers=('NWC', 'WIO', 'NWC'),
                                 feature_group_count=C_CONV) + dw_b
    y = jax.nn.silu(layer_norm(y, ln_g, ln_b))
    return y, full[:, -(CONV_WIDTH - 1):]


def indexer_topk(qi, wi, ki, q_pos, topk):
    s = jnp.einsum('bthd,bsd->bths', qi.astype(jnp.float32), ki.astype(jnp.float32))
    score = jnp.einsum('bths,bth->bts', jax.nn.relu(s), wi.astype(jnp.float32))
    key_pos = jnp.arange(ki.shape[1], dtype=jnp.int32)
    causal = key_pos[None, :] <= q_pos[:, None]
    score = jnp.where(causal[None], score, -jnp.inf)
    _, sel = lax.top_k(score, topk)
    valid = sel <= q_pos[None, :, None]
    return sel, valid


def sparse_attend(q, k_sel, v_sel, valid):
    B, T = q.shape[:2]
    qg = q.reshape(B, T, N_KV_HEADS, GROUP, HEAD_DIM).astype(jnp.float32)
    s = jnp.einsum('btgrd,btkgd->btgrk', qg, k_sel.astype(jnp.float32)) * (HEAD_DIM ** -0.5)
    s = jnp.where(valid[:, :, None, None, :], s, -jnp.inf)
    p = jax.nn.softmax(s, axis=-1)
    o = jnp.einsum('btgrk,btkgd->btgrd', p, v_sel.astype(jnp.float32))
    return o.reshape(B, T, N_HEADS * HEAD_DIM).astype(q.dtype)


def prompt_attention(q, k, v, qi, wi, ki, topk):
    B, S = q.shape[:2]
    nb = S // Q_BLOCK
    bidx = jnp.arange(B)[:, None, None]

    def to_blocks(a):
        return jnp.moveaxis(a.reshape(B, nb, Q_BLOCK, *a.shape[2:]), 1, 0)

    def one_block(args):
        qb, qib, wib, pos_b = args
        sel, valid = indexer_topk(qib, wib, ki, pos_b, topk)
        return sparse_attend(qb, k[bidx, sel], v[bidx, sel], valid)

    pos = jnp.arange(S, dtype=jnp.int32).reshape(nb, Q_BLOCK)
    out = lax.map(one_block, (to_blocks(q), to_blocks(qi), to_blocks(wi), pos))
    return jnp.moveaxis(out, 0, 1).reshape(B, S, N_HEADS * HEAD_DIM)


def sample_attention(layer, q, k_new, v_new, qi, wi, ki_new, q_pos,
                     cache_k, cache_v, cache_idx_k, page_table, topk):
    DB, T = q.shape[:2]
    past = page_table.shape[1] * PAGE_SIZE
    bidx = jnp.arange(DB)[:, None, None]
    ki_past = cache_idx_k[layer, page_table].reshape(DB, past, IDX_DIM)
    ki_all = jnp.concatenate([ki_past, ki_new.astype(ki_past.dtype)], axis=1)
    sel, valid = indexer_topk(qi, wi, ki_all, q_pos, topk)
    in_past = (sel < past)[..., None, None]
    sp = jnp.minimum(sel, past - 1)
    phys = page_table[bidx, sp // PAGE_SIZE]
    off = sp % PAGE_SIZE
    sn = jnp.clip(sel - past, 0, T - 1)
    k_sel = jnp.where(in_past, cache_k[layer, phys, off], k_new[bidx, sn])
    v_sel = jnp.where(in_past, cache_v[layer, phys, off], v_new[bidx, sn])
    return sparse_attend(q, k_sel, v_sel, valid)


def merge_and_ffn(x, conv_out, attn_out, w_out, ffn_norm_g, w_gate, w_up, w_down):
    mix = jnp.concatenate([conv_out, attn_out.astype(conv_out.dtype)], axis=-1)
    x = x + jnp.einsum('bte,ed->btd', mix, w_out)
    h = rms_norm(x, ffn_norm_g)
    f = jax.nn.silu(jnp.einsum('btd,df->btf', h, w_gate)) * jnp.einsum('btd,df->btf', h, w_up)
    return x + jnp.einsum('btf,fd->btd', f, w_down)


def setup_inputs(seed: int = 0) -> dict:
    key = jax.random.key(seed)
    ks = jax.random.split(key, 24)
    f32 = jnp.float32
    n_pages = PAST_LEN // PAGE_SIZE
    n_used = DEC_BATCH * n_pages
    n_pool = n_used + n_used // 4

    def nrm(k, shape, scale=1.0):
        return jax.random.normal(k, shape, f32) * scale

    page_table = jax.random.permutation(ks[0], n_pool)[:n_used].reshape(DEC_BATCH, n_pages).astype(jnp.int32)
    return {
        'x_prompt': nrm(ks[1], (BATCH, SEQ, D_MODEL)),
        'x_sample': nrm(ks[2], (DEC_BATCH, DEC_SEQ, D_MODEL)),
        'cache_k': nrm(ks[3], (DEPTH, n_pool, PAGE_SIZE, N_KV_HEADS, HEAD_DIM)),
        'cache_v': nrm(ks[4], (DEPTH, n_pool, PAGE_SIZE, N_KV_HEADS, HEAD_DIM)),
        'cache_idx_k': nrm(ks[5], (DEPTH, n_pool, PAGE_SIZE, IDX_DIM)),
        'state_conv': nrm(ks[6], (DEPTH, DEC_BATCH, CONV_WIDTH - 1, C_CONV), 0.5),
        'page_table': page_table,
        'attn_norm_g': 1.0 + nrm(ks[7], (DEPTH, D_MODEL), 0.1),
        'w_in': nrm(ks[8], (DEPTH, D_MODEL, D_IN), D_MODEL ** -0.5),
        'q_norm_g': 1.0 + nrm(ks[9], (DEPTH, HEAD_DIM), 0.1),
        'k_norm_g': 1.0 + nrm(ks[10], (DEPTH, HEAD_DIM), 0.1),
        'conv_dw_w': nrm(ks[11], (DEPTH, CONV_WIDTH, C_CONV), CONV_WIDTH ** -0.5),
        'conv_dw_b': nrm(ks[12], (DEPTH, C_CONV), 0.01),
        'conv_ln_g': 1.0 + nrm(ks[13], (DEPTH, C_CONV), 0.1),
        'conv_ln_b': nrm(ks[14], (DEPTH, C_CONV), 0.01),
        'w_out': nrm(ks[15], (DEPTH, D_MIX, D_MODEL), D_MIX ** -0.5),
        'ffn_norm_g': 1.0 + nrm(ks[16], (DEPTH, D_MODEL), 0.1),
        'w_gate': nrm(ks[17], (DEPTH, D_MODEL, D_FF), D_MODEL ** -0.5),
        'w_up': nrm(ks[18], (DEPTH, D_MODEL, D_FF), D_MODEL ** -0.5),
        'w_down': nrm(ks[19], (DEPTH, D_FF, D_MODEL), D_FF ** -0.5),
    }


def reference(x_prompt, x_sample, cache_k, cache_v, cache_idx_k, state_conv, page_table,
              attn_norm_g, w_in, q_norm_g, k_norm_g, conv_dw_w, conv_dw_b, conv_ln_g, conv_ln_b,
              w_out, ffn_norm_g, w_gate, w_up, w_down):
    B, S = x_prompt.shape[:2]
    DB, T = x_sample.shape[:2]
    past = page_table.shape[1] * PAGE_SIZE
    topk_p = min(TOPK_MAX, S // 4)
    topk_s = min(TOPK_MAX, (past + T) // 4)
    pos_p = jnp.arange(S, dtype=jnp.int32)
    pos_s = past + jnp.arange(T, dtype=jnp.int32)

    xp, xs = x_prompt, x_sample
    kp_l, vp_l, ip_l, cp_l, ks_l, vs_l, is_l, cs_l = [], [], [], [], [], [], [], []
    for l in range(DEPTH):
        u, q, k, v, qi, ki, wi = mixer_inputs(xp, pos_p, attn_norm_g[l], w_in[l], q_norm_g[l], k_norm_g[l])
        zero_buf = jnp.zeros((B, CONV_WIDTH - 1, C_CONV), u.dtype)
        conv_out, conv_state = conv_mixer(u, zero_buf, conv_dw_w[l], conv_dw_b[l], conv_ln_g[l], conv_ln_b[l])
        attn_out = prompt_attention(q, k, v, qi, wi, ki, topk_p)
        xp = merge_and_ffn(xp, conv_out, attn_out, w_out[l], ffn_norm_g[l], w_gate[l], w_up[l], w_down[l])
        kp_l.append(k); vp_l.append(v); ip_l.append(ki); cp_l.append(conv_state)

        u, q, k, v, qi, ki, wi = mixer_inputs(xs, pos_s, attn_norm_g[l], w_in[l], q_norm_g[l], k_norm_g[l])
        conv_out, conv_state = conv_mixer(u, state_conv[l].astype(u.dtype), conv_dw_w[l], conv_dw_b[l],
                                          conv_ln_g[l], conv_ln_b[l])
        attn_out = sample_attention(l, q, k, v, qi, wi, ki, pos_s, cache_k, cache_v, cache_idx_k,
                                    page_table, topk_s)
        xs = merge_and_ffn(xs, conv_out, attn_out, w_out[l], ffn_norm_g[l], w_gate[l], w_up[l], w_down[l])
        ks_l.append(k); vs_l.append(v); is_l.append(ki); cs_l.append(conv_state)

    return (xp, xs,
            jnp.stack(kp_l), jnp.stack(vp_l), jnp.stack(ip_l), jnp.stack(cp_l),
            jnp.stack(ks_l), jnp.stack(vs_l), jnp.stack(is_l), jnp.stack(cs_l))
```

```python
import functools

import jax
import jax.numpy as jnp
from jax import lax
from jax.experimental import pallas as pl
from jax.experimental.pallas import tpu as pltpu

F32 = jnp.float32
BF16 = jnp.bfloat16
I32 = jnp.int32

HEAD_DIM = 64
IDX_DIM = 64
N_IDX_HEADS = 16
TOPK_MAX = 256
CONV_WIDTH = 31
ROPE_THETA = 500000.0
ROPE_FRACTION = 4
ROPE_HALF = HEAD_DIM // ROPE_FRACTION // 2
PAGE_SIZE = 128
LANES = 128
CONV_HALO = 32
INT_MIN = -(2 ** 31)
NEG_BIAS = -1e30
MAX_VALUE_PASSES = 48
VMEM_LIMIT = 56 * 1024 * 1024
NT_DIMS = (((1,), (1,)), ((), ()))


def _params(*sem):
    return pltpu.CompilerParams(dimension_semantics=sem, vmem_limit_bytes=VMEM_LIMIT)


def _const_spec(shape):
    zeros = (0,) * len(shape)
    return pl.BlockSpec(shape, lambda *_: zeros)


def _rope_tables(pos):
    ang = pos.astype(F32)[:, None] * (ROPE_THETA ** (-jnp.arange(ROPE_HALF, dtype=F32) / ROPE_HALF))[None, :]
    cos, sin = jnp.cos(ang), jnp.sin(ang)
    t = pos.shape[0]
    rest = HEAD_DIM - 2 * ROPE_HALF
    c = jnp.concatenate([cos, cos, jnp.ones((t, rest), F32)], axis=1)
    a = jnp.concatenate([-sin, jnp.zeros((t, HEAD_DIM - ROPE_HALF), F32)], axis=1)
    b = jnp.concatenate([jnp.zeros((t, ROPE_HALF), F32), sin, jnp.zeros((t, rest), F32)], axis=1)
    rep = LANES // HEAD_DIM
    return tuple(jnp.tile(m, (1, rep)) for m in (c, a, b))


def _proj_kernel(x_ref, g_ref, w_ref, qg_ref, kg_ref, bd_ref, rc_ref, ra_ref, rb_ref,
                 u_ref, q_ref, k_ref, kb_ref, v_ref, vb_ref, qi_ref, ki_ref, kib_ref, wi_ref,
                 *, c_conv, hq, hk, hi):
    x = x_ref[...]
    ms = jnp.mean(x * x, axis=-1, keepdims=True)
    h = ((x * lax.rsqrt(ms + 1e-6)) * g_ref[...]).astype(BF16)
    rc, ra, rb = rc_ref[...], ra_ref[...], rb_ref[...]

    def seg(off, width):
        return jnp.dot(h, w_ref[:, off:off + width], preferred_element_type=F32)

    def rope(z):
        parts = []
        for j in range(z.shape[1] // LANES):
            zj = z[:, j * LANES:(j + 1) * LANES]
            parts.append(zj * rc + pltpu.roll(zj, LANES - ROPE_HALF, 1) * ra + pltpu.roll(zj, ROPE_HALF, 1) * rb)
        return parts[0] if len(parts) == 1 else jnp.concatenate(parts, axis=1)

    def head_norm(z, gain_ref):
        w = z.shape[1]
        s = z * z
        s_hi = s.astype(BF16)
        s_lo = (s - s_hi.astype(F32)).astype(BF16)
        bd = bd_ref[:w, :w]
        ssq = jnp.dot(s_hi, bd, preferred_element_type=F32) + jnp.dot(s_lo, bd, preferred_element_type=F32)
        return (z * lax.rsqrt(ssq * (1.0 / HEAD_DIM) + 1e-6)) * gain_ref[:, :w]

    off = 0
    u_val = seg(off, c_conv)
    off += c_conv
    u_gate = seg(off, c_conv)
    off += c_conv
    u_ref[...] = u_val * jax.nn.sigmoid(u_gate)

    q = rope(head_norm(seg(off, hq), qg_ref))
    off += hq
    q_ref[...] = (q * (HEAD_DIM ** -0.5)).astype(BF16)

    k = rope(head_norm(seg(off, hk), kg_ref))
    off += hk
    k_ref[...] = k
    kb_ref[...] = k.astype(BF16)

    v = seg(off, hk)
    off += hk
    v_ref[...] = v
    vb_ref[...] = v.astype(BF16)

    qi_ref[...] = rope(seg(off, hi)).astype(BF16)
    off += hi

    tail = seg(off, LANES)
    ki = rope(tail)[:, :IDX_DIM]
    ki_ref[...] = ki
    kib_ref[...] = ki.astype(BF16)
    wi_ref[...] = tail[:, IDX_DIM:IDX_DIM + N_IDX_HEADS] * ((N_IDX_HEADS * IDX_DIM) ** -0.5)


def _project(x2d, tabs, g, w_packed, qg, kg, bd, *, tm, c_conv, hq, hk, hi):
    t, d = x2d.shape
    ntab = tabs[0].shape[0] // tm
    row = lambda i: (i, 0)
    tab_spec = pl.BlockSpec((tm, LANES), lambda i: (i % ntab, 0))
    outs = [
        (c_conv, F32), (hq, BF16), (hk, F32), (hk, BF16), (hk, F32), (hk, BF16),
        (hi, BF16), (IDX_DIM, F32), (IDX_DIM, BF16), (N_IDX_HEADS, F32),
    ]
    return pl.pallas_call(
        functools.partial(_proj_kernel, c_conv=c_conv, hq=hq, hk=hk, hi=hi),
        grid=(t // tm,),
        in_specs=[pl.BlockSpec((tm, d), row), _const_spec(g.shape), _const_spec(w_packed.shape),
                  _const_spec(qg.shape), _const_spec(kg.shape), _const_spec(bd.shape),
                  tab_spec, tab_spec, tab_spec],
        out_specs=[pl.BlockSpec((tm, w), row) for w, _ in outs],
        out_shape=[jax.ShapeDtypeStruct((t, w), dt) for w, dt in outs],
        compiler_params=_params("parallel"),
        name="project",
    )(x2d, g, w_packed, qg, kg, bd, *tabs)


def _ln_swish(y, g, b):
    mu = jnp.mean(y, axis=-1, keepdims=True)
    yc = y - mu
    var = jnp.mean(yc * yc, axis=-1, keepdims=True)
    yn = (yc * lax.rsqrt(var + 1e-5)) * g + b
    return yn * jax.nn.sigmoid(yn)


def _conv_prompt_kernel(u_ref, up_ref, w_ref, b_ref, g_ref, beta_ref, o_ref, full_ref, *, ts, rt):
    i = pl.program_id(1)
    full_ref[0:CONV_HALO, :] = jnp.where(i > 0, up_ref[0], 0.0)
    full_ref[CONV_HALO:, :] = u_ref[0]
    base = CONV_HALO - (CONV_WIDTH - 1)
    for r0 in range(0, ts, rt):
        acc = jnp.zeros((rt, u_ref.shape[2]), F32)
        for j in range(CONV_WIDTH):
            acc = acc + full_ref[pl.ds(r0 + base + j, rt), :] * w_ref[j:j + 1, :]
        y = acc + b_ref[...]
        o_ref[0, r0:r0 + rt, :] = _ln_swish(y, g_ref[...], beta_ref[...]).astype(BF16)


def _conv_prompt(u, w, b, g, beta, *, ts, rt):
    bsz, s, c = u.shape
    per = ts // CONV_HALO
    return pl.pallas_call(
        functools.partial(_conv_prompt_kernel, ts=ts, rt=rt),
        grid=(bsz, s // ts),
        in_specs=[pl.BlockSpec((1, ts, c), lambda bi, i: (bi, i, 0)),
                  pl.BlockSpec((1, CONV_HALO, c), lambda bi, i: (bi, jnp.maximum(i * per - 1, 0), 0)),
                  _const_spec(w.shape), _const_spec(b.shape), _const_spec(g.shape), _const_spec(beta.shape)],
        out_specs=pl.BlockSpec((1, ts, c), lambda bi, i: (bi, i, 0)),
        out_shape=jax.ShapeDtypeStruct((bsz, s, c), BF16),
        scratch_shapes=[pltpu.VMEM((ts + CONV_HALO, c), F32)],
        compiler_params=_params("parallel", "parallel"),
        name="conv_prompt",
    )(u, u, w, b, g, beta)


def _conv_sample_kernel(full_ref, w_ref, b_ref, g_ref, beta_ref, o_ref, *, bb, t):
    for bi in range(bb):
        acc = jnp.zeros((t, full_ref.shape[2]), F32)
        for j in range(CONV_WIDTH):
            acc = acc + full_ref[bi, pl.ds(j, t), :] * w_ref[j:j + 1, :]
        y = acc + b_ref[...]
        o_ref[bi] = _ln_swish(y, g_ref[...], beta_ref[...])


def _conv_sample(full, w, b, g, beta, *, bb, t):
    db, rows, c = full.shape
    return pl.pallas_call(
        functools.partial(_conv_sample_kernel, bb=bb, t=t),
        grid=(db // bb,),
        in_specs=[pl.BlockSpec((bb, rows, c), lambda i: (i, 0, 0)),
                  _const_spec(w.shape), _const_spec(b.shape), _const_spec(g.shape), _const_spec(beta.shape)],
        out_specs=pl.BlockSpec((bb, t, c), lambda i: (i, 0, 0)),
        out_shape=jax.ShapeDtypeStruct((db, t, c), F32),
        compiler_params=_params("parallel"),
        name="conv_sample",
    )(full, w, b, g, beta)


def _key_to_score(key):
    return lax.bitcast_convert_type(jnp.where(key < 0, key ^ 0x7FFFFFFF, key), F32)


def _any(flag):
    return jnp.max(jnp.where(flag, 1.0, 0.0)) > 0.0


def _topk_cut(count, smin, smax, topk, idx_bits):
    kf = float(topk)
    n_adm = count(lambda s, idx: s > -jnp.inf)
    n_max = count(lambda s, idx: s >= smax)
    few = n_adm < kf
    lo = jnp.where(few, -jnp.inf, jnp.where(n_max >= kf, smax, smin))
    c_lo = jnp.where(n_max >= kf, n_max, n_adm)

    def probe(lo, hi, c_lo):
        cand = lo * 0.5 + hi * 0.5
        return cand, (cand > lo) & (cand < hi) & (c_lo != kf)

    def cond(st):
        return (st[0] < MAX_VALUE_PASSES) & st[4]

    def body(st):
        p, lo, hi, c_lo, _ = st
        cand, act = probe(lo, hi, c_lo)
        n = count(lambda s, idx: s >= cand)
        up = act & (n >= kf)
        down = act & (n < kf)
        lo, c_lo, hi = jnp.where(up, cand, lo), jnp.where(up, n, c_lo), jnp.where(down, cand, hi)
        return p + 1, lo, hi, c_lo, _any(probe(lo, hi, c_lo)[1])

    st = (jnp.int32(0), lo, smax, c_lo, _any(probe(lo, smax, c_lo)[1]))
    _, lo, _, _, _ = lax.while_loop(cond, body, st)

    n_gt = count(lambda s, idx: s > lo)
    n_ge = count(lambda s, idx: s >= lo)
    ok = (n_gt < kf) & (n_ge >= kf)

    def bit_search():
        zero = jnp.zeros_like(lo)
        tau = jnp.where(count(lambda s, idx: s >= zero) >= kf, 0, INT_MIN).astype(I32)

        def bit_body(b, tau):
            cand = tau | lax.shift_left(jnp.int32(1), jnp.int32(30) - b)
            cand_f = _key_to_score(cand)
            return jnp.where(count(lambda s, idx: s >= cand_f) >= kf, cand, tau)

        tau_f = _key_to_score(lax.fori_loop(0, 31, bit_body, tau))
        tau_f = jnp.where(ok, lo, tau_f)
        return tau_f, count(lambda s, idx: s > tau_f), count(lambda s, idx: s >= tau_f)

    tau, n_gt, n_ge = lax.cond(_any(~ok), bit_search, lambda: (lo, n_gt, n_ge))
    need = kf - n_gt

    def tie_search():
        def body(b, p):
            cand = p + lax.shift_left(jnp.int32(1), jnp.int32(idx_bits - 1) - b)
            n = count(lambda s, idx: (s == tau) & (idx <= cand))
            return jnp.where(n < need, cand, p)
        return lax.fori_loop(0, idx_bits, body, jnp.full(lo.shape, -1, I32)) + 1

    jlim = lax.cond(_any(n_ge > kf), tie_search, lambda: jnp.full(lo.shape, 2 ** idx_bits, I32))
    return tau, jlim


def _selected(score, kidx, tau, jlim):
    return (score > tau) | ((score == tau) & (kidx <= jlim))


def _canonical(score):
    return jnp.where(score == 0.0, 0.0, score)


def _sublane_fold(x, op):
    return op(x.reshape(x.shape[0] // 8, 8, x.shape[1]), axis=0)


def _pattn_kernel(q_ref, qi_ref, wi_ref, ki_ref, k_ref, vt_ref, o_ref, sc_ref, acc_ref,
                  *, tq, topk, n_kv, group, idx_bits):
    i = pl.program_id(1)
    n_chunks = i + 1
    krow = lax.broadcasted_iota(I32, (tq, tq), 0)
    qcol = lax.broadcasted_iota(I32, (tq, tq), 1)

    def index_chunk(c, carry):
        mn, mx = carry
        start = pl.multiple_of(c * tq, tq)
        kic = ki_ref[0, pl.ds(start, tq), :]
        acc = jnp.zeros((tq, tq), F32)
        for h in range(N_IDX_HEADS):
            x = lax.dot_general(kic, qi_ref[0, h], NT_DIMS, preferred_element_type=F32)
            acc = acc + wi_ref[0, h:h + 1, :] * jnp.maximum(x, 0.0)
        acc = _canonical(acc)
        future = (c == i) & (krow > qcol)
        sc_ref[c] = jnp.where(future, -jnp.inf, acc)
        mn = jnp.minimum(mn, _sublane_fold(jnp.where(future, jnp.inf, acc), jnp.min))
        mx = jnp.maximum(mx, _sublane_fold(jnp.where(future, -jnp.inf, acc), jnp.max))
        return mn, mx

    mn, mx = lax.fori_loop(0, n_chunks, index_chunk,
                           (jnp.full((8, tq), jnp.inf, F32), jnp.full((8, tq), -jnp.inf, F32)))
    smin = jnp.min(mn, axis=0, keepdims=True)
    smax = jnp.max(mx, axis=0, keepdims=True)

    def count(pred):
        def body(c, cnt):
            hit = pred(sc_ref[c], c * tq + krow)
            return cnt + _sublane_fold(jnp.where(hit, 1.0, 0.0), jnp.sum)
        cnt = lax.fori_loop(0, n_chunks, body, jnp.zeros((8, tq), F32))
        return jnp.sum(cnt, axis=0, keepdims=True)

    tau, jlim = _topk_cut(count, smin, smax, topk, idx_bits)

    acc_ref[...] = jnp.zeros(acc_ref.shape, F32)
    gw = group * tq

    def attend_chunk(c, carry):
        start = pl.multiple_of(c * tq, tq)
        kidx = c * tq + krow
        sel = _selected(sc_ref[c], kidx, tau, jlim) & (kidx <= i * tq + qcol)
        bias = jnp.where(sel, 0.0, NEG_BIAS)
        bias = jnp.concatenate([bias] * group, axis=1)
        new = []
        for g in range(n_kv):
            m_old, l_old = carry[g]
            kg = k_ref[0, g, pl.ds(start, tq), :]
            qg = q_ref[0, g * group:(g + 1) * group].reshape(gw, HEAD_DIM)
            s = lax.dot_general(kg, qg, NT_DIMS, preferred_element_type=F32) + bias
            m_new = jnp.maximum(m_old, jnp.max(_sublane_fold(s, jnp.max), axis=0, keepdims=True))
            p = jnp.exp(s - m_new)
            alpha = jnp.exp(m_old - m_new)
            l_new = alpha * l_old + jnp.sum(_sublane_fold(p, jnp.sum), axis=0, keepdims=True)
            pv = jnp.dot(vt_ref[0, c, g * HEAD_DIM:(g + 1) * HEAD_DIM, :], p.astype(BF16),
                         preferred_element_type=F32)
            acc_ref[g] = alpha * acc_ref[g] + pv
            new.append((m_new, l_new))
        return tuple(new)

    init = tuple((jnp.full((1, gw), NEG_BIAS, F32), jnp.zeros((1, gw), F32)) for _ in range(n_kv))
    fin = lax.fori_loop(0, n_chunks, attend_chunk, init)

    outs = []
    for g in range(n_kv):
        og = acc_ref[g] / fin[g][1]
        for r in range(group):
            outs.append(og[:, r * tq:(r + 1) * tq])
    o_ref[0] = jnp.concatenate(outs, axis=0).T.astype(BF16)


def _prompt_attention(q, qi, wi_t, kib, kb, vt, *, tq, topk, n_kv, group):
    bsz, n_heads, s, _ = q.shape
    hk = n_kv * HEAD_DIM
    hq = n_heads * HEAD_DIM
    return pl.pallas_call(
        functools.partial(_pattn_kernel, tq=tq, topk=topk, n_kv=n_kv, group=group,
                          idx_bits=max(s - 1, 1).bit_length()),
        grid=(bsz, s // tq),
        in_specs=[pl.BlockSpec((1, n_heads, tq, HEAD_DIM), lambda bi, i: (bi, 0, i, 0)),
                  pl.BlockSpec((1, N_IDX_HEADS, tq, IDX_DIM), lambda bi, i: (bi, 0, i, 0)),
                  pl.BlockSpec((1, N_IDX_HEADS, tq), lambda bi, i: (bi, 0, i)),
                  pl.BlockSpec((1, s, IDX_DIM), lambda bi, i: (bi, 0, 0)),
                  pl.BlockSpec((1, n_kv, s, HEAD_DIM), lambda bi, i: (bi, 0, 0, 0)),
                  pl.BlockSpec((1, s // tq, hk, tq), lambda bi, i: (bi, 0, 0, 0))],
        out_specs=pl.BlockSpec((1, tq, hq), lambda bi, i: (bi, i, 0)),
        out_shape=jax.ShapeDtypeStruct((bsz, s, hq), BF16),
        scratch_shapes=[pltpu.VMEM((s // tq, tq, tq), F32),
                        pltpu.VMEM((n_kv, HEAD_DIM, group * tq), F32)],
        compiler_params=_params("parallel", "arbitrary"),
        name="prompt_attention",
    )(q, qi, wi_t, kib, kb, vt)


def _lane_fold(m, op):
    out = m[:, :LANES]
    for j in range(1, m.shape[1] // LANES):
        out = op(out, m[:, j * LANES:(j + 1) * LANES])
    return out


def _sattn_kernel(pt_ref, qbd_ref, qi_ref, wi_ref, kin_ref, kn_ref, vn_ref, cidx_ref, ck_ref, cv_ref,
                  o_ref, idx_buf, k_buf, v_buf, sc_ref, scn_ref, out_ref, sem,
                  *, t, kc, n_pages, topk, n_kv, group, idx_bits):
    b = pl.program_id(0)
    past = n_pages * PAGE_SIZE
    n_chunks = past // kc
    n_heads = n_kv * group
    rows_a = n_heads * t

    def page_copies(p):
        page = pt_ref[b, p]
        dst = pl.ds(pl.multiple_of(p * PAGE_SIZE, PAGE_SIZE), PAGE_SIZE)
        return (pltpu.make_async_copy(cidx_ref.at[page], idx_buf.at[dst, :], sem.at[0]),
                pltpu.make_async_copy(ck_ref.at[page], k_buf.at[dst, :], sem.at[1]),
                pltpu.make_async_copy(cv_ref.at[page], v_buf.at[dst, :], sem.at[2]))

    def start_page(p, carry):
        for cp in page_copies(p):
            cp.start()
        return carry

    def wait_page(p, carry):
        for cp in page_copies(p):
            cp.wait()
        return carry

    lax.fori_loop(0, n_pages, start_page, 0)
    lax.fori_loop(0, n_pages, wait_page, 0)

    qi = qi_ref[0]
    wi = wi_ref[0]

    def head_sum(y):
        out = y[0:t]
        for h in range(1, N_IDX_HEADS):
            out = out + y[h * t:(h + 1) * t]
        return out

    def index_chunk(c, carry):
        mn, mx = carry
        start = pl.multiple_of(c * kc, kc)
        kic = idx_buf[pl.ds(start, kc), :].astype(BF16)
        x = lax.dot_general(qi, kic, NT_DIMS, preferred_element_type=F32)
        sc = _canonical(head_sum(wi * jnp.maximum(x, 0.0)))
        sc_ref[c] = sc
        return jnp.minimum(mn, _lane_fold(sc, jnp.minimum)), jnp.maximum(mx, _lane_fold(sc, jnp.maximum))

    mn, mx = lax.fori_loop(0, n_chunks, index_chunk,
                           (jnp.full((t, LANES), jnp.inf, F32), jnp.full((t, LANES), -jnp.inf, F32)))

    rown = lax.broadcasted_iota(I32, (t, LANES), 0)
    coln = lax.broadcasted_iota(I32, (t, LANES), 1)
    xn = lax.dot_general(qi, kin_ref[0], NT_DIMS, preferred_element_type=F32)
    scn = _canonical(head_sum(wi * jnp.maximum(xn, 0.0)))
    scn_ref[...] = jnp.where(coln <= rown, scn, -jnp.inf)
    smin = jnp.min(jnp.minimum(mn, jnp.where(coln <= rown, scn, jnp.inf)), axis=1, keepdims=True)
    smax = jnp.max(jnp.maximum(mx, jnp.where(coln <= rown, scn, -jnp.inf)), axis=1, keepdims=True)

    colc = lax.broadcasted_iota(I32, (t, kc), 1)

    def count(pred):
        def body(c, cnt):
            hit = pred(sc_ref[c], c * kc + colc)
            return cnt + _lane_fold(jnp.where(hit, 1.0, 0.0), jnp.add)
        cnt = lax.fori_loop(0, n_chunks, body, jnp.zeros((t, LANES), F32))
        cnt = cnt + jnp.where(pred(scn_ref[...], past + coln), 1.0, 0.0)
        return jnp.sum(cnt, axis=1, keepdims=True)

    tau, jlim = _topk_cut(count, smin, smax, topk, idx_bits)

    def bias_of(sc, kpos):
        return jnp.where(_selected(sc, kpos, tau, jlim), 0.0, NEG_BIAS)

    qbd = qbd_ref[0]

    def softmax_step(state, s, vals):
        m_old, l_old, acc = state
        m_new = jnp.maximum(m_old, jnp.max(s, axis=1, keepdims=True))
        p = jnp.exp(s - m_new)
        alpha = jnp.exp(m_old - m_new)
        l_new = alpha * l_old + jnp.sum(p, axis=1, keepdims=True)
        acc = alpha * acc + jnp.dot(p.astype(BF16), vals, preferred_element_type=F32)
        return m_new, l_new, acc

    def attend_chunk(c, state):
        start = pl.multiple_of(c * kc, kc)
        keys = k_buf[pl.ds(start, kc), :].astype(BF16)
        vals = v_buf[pl.ds(start, kc), :].astype(BF16)
        bias = jnp.concatenate([bias_of(sc_ref[c], c * kc + colc)] * n_heads, axis=0)
        s = lax.dot_general(qbd, keys, NT_DIMS, preferred_element_type=F32) + bias
        return softmax_step(state, s, vals)

    hk = n_kv * HEAD_DIM
    state = (jnp.full((rows_a, 1), NEG_BIAS, F32), jnp.zeros((rows_a, 1), F32), jnp.zeros((rows_a, hk), F32))
    state = lax.fori_loop(0, n_chunks, attend_chunk, state)
    biasn = jnp.where(coln <= rown, bias_of(scn_ref[...], past + coln), NEG_BIAS)
    sn = lax.dot_general(qbd, kn_ref[0], NT_DIMS, preferred_element_type=F32)
    sn = sn + jnp.concatenate([biasn] * n_heads, axis=0)
    _, l_fin, acc = softmax_step(state, sn, vn_ref[0])
    o = acc / l_fin
    for h in range(n_heads):
        g = h // group
        out_ref[:, h * HEAD_DIM:(h + 1) * HEAD_DIM] = o[h * t:(h + 1) * t, g * HEAD_DIM:(g + 1) * HEAD_DIM]
    o_ref[0] = out_ref[...]


def _sample_attention(page_table, qbd, qi, wi, ki_new, k_new, v_new, cache_idx, cache_k, cache_v,
                      *, t, kc, topk, n_kv, group):
    db, n_pages = page_table.shape
    past = n_pages * PAGE_SIZE
    hk = n_kv * HEAD_DIM
    hq = n_kv * group * HEAD_DIM
    blk = lambda b, pt: (b, 0, 0)
    any_spec = pl.BlockSpec(memory_space=pl.ANY)
    grid_spec = pltpu.PrefetchScalarGridSpec(
        num_scalar_prefetch=1,
        grid=(db,),
        in_specs=[pl.BlockSpec((1,) + qbd.shape[1:], blk), pl.BlockSpec((1,) + qi.shape[1:], blk),
                  pl.BlockSpec((1,) + wi.shape[1:], blk), pl.BlockSpec((1,) + ki_new.shape[1:], blk),
                  pl.BlockSpec((1,) + k_new.shape[1:], blk), pl.BlockSpec((1,) + v_new.shape[1:], blk),
                  any_spec, any_spec, any_spec],
        out_specs=pl.BlockSpec((1, t, hq), blk),
        scratch_shapes=[pltpu.VMEM((past, IDX_DIM), F32), pltpu.VMEM((past, hk), F32), pltpu.VMEM((past, hk), F32),
                        pltpu.VMEM((past // kc, t, kc), F32), pltpu.VMEM((t, LANES), F32),
                        pltpu.VMEM((t, hq), F32), pltpu.SemaphoreType.DMA((3,))],
    )
    return pl.pallas_call(
        functools.partial(_sattn_kernel, t=t, kc=kc, n_pages=n_pages, topk=topk, n_kv=n_kv, group=group,
                          idx_bits=max(past + LANES - 1, 1).bit_length()),
        grid_spec=grid_spec,
        out_shape=jax.ShapeDtypeStruct((db, t, hq), F32),
        compiler_params=_params("arbitrary"),
        name="sample_attention",
    )(page_table, qbd, qi, wi, ki_new, k_new, v_new, cache_idx, cache_k, cache_v)


def _merge_ffn_kernel(x_ref, conv_ref, attn_ref, wo_ref, g_ref, wg_ref, wu_ref, wd_ref, o_ref, *, c_conv):
    conv = conv_ref[...].astype(BF16)
    attn = attn_ref[...].astype(BF16)
    x = (x_ref[...] + jnp.dot(conv, wo_ref[:c_conv, :], preferred_element_type=F32)
         + jnp.dot(attn, wo_ref[c_conv:, :], preferred_element_type=F32))
    ms = jnp.mean(x * x, axis=-1, keepdims=True)
    h = ((x * lax.rsqrt(ms + 1e-6)) * g_ref[...]).astype(BF16)
    ffn = None
    for ci in range(wg_ref.shape[0]):
        gate = jnp.dot(h, wg_ref[ci], preferred_element_type=F32)
        up = jnp.dot(h, wu_ref[ci], preferred_element_type=F32)
        f = ((gate * jax.nn.sigmoid(gate)) * up).astype(BF16)
        part = jnp.dot(f, wd_ref[ci], preferred_element_type=F32)
        ffn = part if ffn is None else ffn + part
    o_ref[...] = x + ffn


def _merge_ffn(x2d, conv, attn, wo, g, wg, wu, wd, *, tm):
    t, d = x2d.shape
    c_conv = conv.shape[1]
    row = lambda i: (i, 0)
    return pl.pallas_call(
        functools.partial(_merge_ffn_kernel, c_conv=c_conv),
        grid=(t // tm,),
        in_specs=[pl.BlockSpec((tm, d), row), pl.BlockSpec((tm, c_conv), row), pl.BlockSpec((tm, attn.shape[1]), row),
                  _const_spec(wo.shape), _const_spec(g.shape), _const_spec(wg.shape), _const_spec(wu.shape),
                  _const_spec(wd.shape)],
        out_specs=pl.BlockSpec((tm, d), row),
        out_shape=jax.ShapeDtypeStruct((t, d), F32),
        compiler_params=_params("parallel"),
        name="merge_ffn",
    )(x2d, conv, attn, wo, g, wg, wu, wd)


def _ffn_chunks(d_ff):
    for n in (2, 3, 4):
        if d_ff % (n * LANES) == 0:
            return n
    return 1


def kernel(x_prompt, x_sample, cache_k, cache_v, cache_idx_k, state_conv, page_table, attn_norm_g, w_in,
           q_norm_g, k_norm_g, conv_dw_w, conv_dw_b, conv_ln_g, conv_ln_b, w_out, ffn_norm_g, w_gate, w_up,
           w_down):
    bsz, s, d = x_prompt.shape
    db, t, _ = x_sample.shape
    depth = w_in.shape[0]
    assert depth == 1, "single-layer step"
    n_pool = cache_k.shape[1]
    n_kv = cache_k.shape[3]
    c_conv = state_conv.shape[3]
    hk = n_kv * HEAD_DIM
    hq = w_out.shape[1] - c_conv
    group = hq // hk
    hi = N_IDX_HEADS * IDX_DIM
    d_main = 2 * c_conv + hq + 2 * hk + hi
    d_ff = w_gate.shape[2]
    past = page_table.shape[1] * PAGE_SIZE
    assert w_in.shape[2] == d_main + IDX_DIM + N_IDX_HEADS
    assert c_conv % LANES == 0 and hq % LANES == 0 and hk % LANES == 0
    topk_p = min(TOPK_MAX, s // 4)
    topk_s = min(TOPK_MAX, (past + t) // 4)

    tm = min(512, s)
    tq = min(256, s)
    assert s % tm == 0 and s % tq == 0 and (db * t) % min(tm, db * t) == 0 and tq >= topk_p

    w_tail = jnp.pad(w_in[0, :, d_main:], ((0, 0), (0, LANES - IDX_DIM - N_IDX_HEADS)))
    w_packed = jnp.concatenate([w_in[0, :, :d_main], w_tail], axis=1).astype(BF16)
    qg = jnp.tile(q_norm_g, (1, hq // HEAD_DIM))
    kg = jnp.tile(k_norm_g, (1, hk // HEAD_DIM))
    head_of = jnp.arange(hq) // HEAD_DIM
    bd = (head_of[:, None] == head_of[None, :]).astype(BF16)
    wo = w_out[0].astype(BF16)
    n_fc = _ffn_chunks(d_ff)
    col_chunks = lambda w: w.reshape(d, n_fc, d_ff // n_fc).transpose(1, 0, 2).astype(BF16)
    wg, wu = col_chunks(w_gate[0]), col_chunks(w_up[0])
    wd = w_down[0].reshape(n_fc, d_ff // n_fc, d).astype(BF16)
    dims = dict(c_conv=c_conv, hq=hq, hk=hk, hi=hi)

    xp2 = x_prompt.reshape(bsz * s, d)
    tabs_p = _rope_tables(jnp.arange(s, dtype=I32))
    u, q, k, kb, v, vb, qi, ki, kib, wi = _project(xp2, tabs_p, attn_norm_g, w_packed, qg, kg, bd, tm=tm, **dims)
    u3 = u.reshape(bsz, s, c_conv)
    conv_p = _conv_prompt(u3, conv_dw_w[0], conv_dw_b, conv_ln_g, conv_ln_b, ts=min(256, s), rt=64)
    n_heads = hq // HEAD_DIM
    heads_major = lambda a, n: a.reshape(bsz, s, n, HEAD_DIM).transpose(0, 2, 1, 3)
    vt = vb.reshape(bsz, s // tq, tq, hk).transpose(0, 1, 3, 2)
    attn_p = _prompt_attention(heads_major(q, n_heads), heads_major(qi, N_IDX_HEADS),
                               wi.reshape(bsz, s, N_IDX_HEADS).transpose(0, 2, 1),
                               kib.reshape(bsz, s, IDX_DIM), heads_major(kb, n_kv), vt,
                               tq=tq, topk=topk_p, n_kv=n_kv, group=group)
    y_p = _merge_ffn(xp2, conv_p.reshape(bsz * s, c_conv), attn_p.reshape(bsz * s, hq), wo, ffn_norm_g,
                     wg, wu, wd, tm=tm)

    ts_tok = db * t
    tms = min(tm, ts_tok)
    xs2 = x_sample.reshape(ts_tok, d)
    pos_s = past + (jnp.arange(tms, dtype=I32) % t)
    tabs_s = _rope_tables(pos_s)
    us, qs, ks, ksb, vs, vsb, qis, kis, kisb, wis = _project(xs2, tabs_s, attn_norm_g, w_packed, qg, kg, bd,
                                                             tm=tms, **dims)
    full_s = jnp.concatenate([state_conv[0], us.reshape(db, t, c_conv)], axis=1)
    conv_s = _conv_sample(full_s, conv_dw_w[0], conv_dw_b, conv_ln_g, conv_ln_b, bb=min(8, db), t=t)

    q_ht = qs.reshape(db, t, n_heads, HEAD_DIM).transpose(0, 2, 1, 3)
    kv_of = jnp.arange(n_heads) // group
    onehot = (kv_of[:, None] == jnp.arange(n_kv)[None, :]).astype(BF16)
    qbd = (q_ht[:, :, :, None, :] * onehot[None, :, None, :, None]).reshape(db, n_heads * t, hk)
    qi_ht = qis.reshape(db, t, N_IDX_HEADS, IDX_DIM).transpose(0, 2, 1, 3).reshape(db, N_IDX_HEADS * t, IDX_DIM)
    wi_ht = wis.reshape(db, t, N_IDX_HEADS).transpose(0, 2, 1).reshape(db, N_IDX_HEADS * t, 1)
    pad_rows = lambda a: jnp.pad(a.reshape(db, t, a.shape[-1]), ((0, 0), (0, LANES - t), (0, 0)))
    attn_s = _sample_attention(page_table, qbd, qi_ht, wi_ht, pad_rows(kisb), pad_rows(ksb), pad_rows(vsb),
                               cache_idx_k[0], cache_k[0].reshape(n_pool, PAGE_SIZE, hk),
                               cache_v[0].reshape(n_pool, PAGE_SIZE, hk),
                               t=t, kc=min(512, past), topk=topk_s, n_kv=n_kv, group=group)
    y_s = _merge_ffn(xs2, conv_s.reshape(ts_tok, c_conv), attn_s.reshape(ts_tok, hq), wo, ffn_norm_g,
                     wg, wu, wd, tm=tms)

    return (y_p.reshape(bsz, s, d), y_s.reshape(db, t, d),
            k.reshape(1, bsz, s, n_kv, HEAD_DIM), v.reshape(1, bsz, s, n_kv, HEAD_DIM),
            ki.reshape(1, bsz, s, IDX_DIM), u3[:, s - (CONV_WIDTH - 1):][None],
            ks.reshape(1, db, t, n_kv, HEAD_DIM), vs.reshape(1, db, t, n_kv, HEAD_DIM),
            kis.reshape(1, db, t, IDX_DIM), full_s[:, t:][None])
```

```python
import functools

import jax
import jax.numpy as jnp
from jax import lax
from jax.experimental import pallas as pl
from jax.experimental.pallas import tpu as pltpu

F32 = jnp.float32
BF16 = jnp.bfloat16
I32 = jnp.int32

HEAD_DIM = 64
IDX_DIM = 64
N_IDX_HEADS = 16
TOPK_MAX = 256
CONV_WIDTH = 31
ROPE_THETA = 500000.0
ROPE_FRACTION = 4
ROPE_HALF = HEAD_DIM // ROPE_FRACTION // 2
PAGE_SIZE = 128
LANES = 128
CONV_HALO = 32
INT_MIN = -(2 ** 31)
NEG_BIAS = -1e30
MAX_VALUE_PASSES = 48
VMEM_LIMIT = 56 * 1024 * 1024
NT_DIMS = (((1,), (1,)), ((), ()))


def _params(*sem):
    return pltpu.CompilerParams(dimension_semantics=sem, vmem_limit_bytes=VMEM_LIMIT)


def _const_spec(shape):
    zeros = (0,) * len(shape)
    return pl.BlockSpec(shape, lambda *_: zeros)


def _rope_tables(pos):
    ang = pos.astype(F32)[:, None] * (ROPE_THETA ** (-jnp.arange(ROPE_HALF, dtype=F32) / ROPE_HALF))[None, :]
    cos, sin = jnp.cos(ang), jnp.sin(ang)
    t = pos.shape[0]
    rest = HEAD_DIM - 2 * ROPE_HALF
    c = jnp.concatenate([cos, cos, jnp.ones((t, rest), F32)], axis=1)
    a = jnp.concatenate([-sin, jnp.zeros((t, HEAD_DIM - ROPE_HALF), F32)], axis=1)
    b = jnp.concatenate([jnp.zeros((t, ROPE_HALF), F32), sin, jnp.zeros((t, rest), F32)], axis=1)
    rep = LANES // HEAD_DIM
    return tuple(jnp.tile(m, (1, rep)) for m in (c, a, b))


def _proj_kernel(x_ref, g_ref, w_ref, qg_ref, kg_ref, bd_ref, rc_ref, ra_ref, rb_ref,
                 u_ref, q_ref, k_ref, kb_ref, v_ref, vb_ref, qi_ref, ki_ref, kib_ref, wi_ref,
                 *, c_conv, hq, hk, hi):
    x = x_ref[...]
    ms = jnp.mean(x * x, axis=-1, keepdims=True)
    h = ((x * lax.rsqrt(ms + 1e-6)) * g_ref[...]).astype(BF16)
    rc, ra, rb = rc_ref[...], ra_ref[...], rb_ref[...]

    def seg(off, width):
        return jnp.dot(h, w_ref[:, off:off + width], preferred_element_type=F32)

    def rope(z):
        parts = []
        for j in range(z.shape[1] // LANES):
            zj = z[:, j * LANES:(j + 1) * LANES]
            parts.append(zj * rc + pltpu.roll(zj, LANES - ROPE_HALF, 1) * ra + pltpu.roll(zj, ROPE_HALF, 1) * rb)
        return parts[0] if len(parts) == 1 else jnp.concatenate(parts, axis=1)

    def head_norm(z, gain_ref):
        w = z.shape[1]
        s = z * z
        s_hi = s.astype(BF16)
        s_lo = (s - s_hi.astype(F32)).astype(BF16)
        bd = bd_ref[:w, :w]
        ssq = jnp.dot(s_hi, bd, preferred_element_type=F32) + jnp.dot(s_lo, bd, preferred_element_type=F32)
        return (z * lax.rsqrt(ssq * (1.0 / HEAD_DIM) + 1e-6)) * gain_ref[:, :w]

    off = 0
    u_val = seg(off, c_conv)
    off += c_conv
    u_gate = seg(off, c_conv)
    off += c_conv
    u_ref[...] = u_val * jax.nn.sigmoid(u_gate)

    q = rope(head_norm(seg(off, hq), qg_ref))
    off += hq
    q_ref[...] = (q * (HEAD_DIM ** -0.5)).astype(BF16)

    k = rope(head_norm(seg(off, hk), kg_ref))
    off += hk
    k_ref[...] = k
    kb_ref[...] = k.astype(BF16)

    v = seg(off, hk)
    off += hk
    v_ref[...] = v
    vb_ref[...] = v.astype(BF16)

    qi_ref[...] = rope(seg(off, hi)).astype(BF16)
    off += hi

    tail = seg(off, LANES)
    ki = rope(tail)[:, :IDX_DIM]
    ki_ref[...] = ki
    kib_ref[...] = ki.astype(BF16)
    wi_ref[...] = tail[:, IDX_DIM:IDX_DIM + N_IDX_HEADS] * ((N_IDX_HEADS * IDX_DIM) ** -0.5)


def _project(x2d, tabs, g, w_packed, qg, kg, bd, *, tm, c_conv, hq, hk, hi):
    t, d = x2d.shape
    ntab = tabs[0].shape[0] // tm
    row = lambda i: (i, 0)
    tab_spec = pl.BlockSpec((tm, LANES), lambda i: (i % ntab, 0))
    outs = [
        (c_conv, F32), (hq, BF16), (hk, F32), (hk, BF16), (hk, F32), (hk, BF16),
        (hi, BF16), (IDX_DIM, F32), (IDX_DIM, BF16), (N_IDX_HEADS, F32),
    ]
    return pl.pallas_call(
        functools.partial(_proj_kernel, c_conv=c_conv, hq=hq, hk=hk, hi=hi),
        grid=(t // tm,),
        in_specs=[pl.BlockSpec((tm, d), row), _const_spec(g.shape), _const_spec(w_packed.shape),
                  _const_spec(qg.shape), _const_spec(kg.shape), _const_spec(bd.shape),
                  tab_spec, tab_spec, tab_spec],
        out_specs=[pl.BlockSpec((tm, w), row) for w, _ in outs],
        out_shape=[jax.ShapeDtypeStruct((t, w), dt) for w, dt in outs],
        compiler_params=_params("parallel"),
        name="project",
    )(x2d, g, w_packed, qg, kg, bd, *tabs)


def _ln_swish(y, g, b):
    mu = jnp.mean(y, axis=-1, keepdims=True)
    yc = y - mu
    var = jnp.mean(yc * yc, axis=-1, keepdims=True)
    yn = (yc * lax.rsqrt(var + 1e-5)) * g + b
    return yn * jax.nn.sigmoid(yn)


def _conv_prompt_kernel(u_ref, up_ref, w_ref, b_ref, g_ref, beta_ref, o_ref, full_ref, *, ts, rt):
    i = pl.program_id(1)
    full_ref[0:CONV_HALO, :] = jnp.where(i > 0, up_ref[0], 0.0)
    full_ref[CONV_HALO:, :] = u_ref[0]
    base = CONV_HALO - (CONV_WIDTH - 1)
    for r0 in range(0, ts, rt):
        acc = jnp.zeros((rt, u_ref.shape[2]), F32)
        for j in range(CONV_WIDTH):
            acc = acc + full_ref[pl.ds(r0 + base + j, rt), :] * w_ref[j:j + 1, :]
        y = acc + b_ref[...]
        o_ref[0, r0:r0 + rt, :] = _ln_swish(y, g_ref[...], beta_ref[...]).astype(BF16)


def _conv_prompt(u, w, b, g, beta, *, ts, rt):
    bsz, s, c = u.shape
    per = ts // CONV_HALO
    return pl.pallas_call(
        functools.partial(_conv_prompt_kernel, ts=ts, rt=rt),
        grid=(bsz, s // ts),
        in_specs=[pl.BlockSpec((1, ts, c), lambda bi, i: (bi, i, 0)),
                  pl.BlockSpec((1, CONV_HALO, c), lambda bi, i: (bi, jnp.maximum(i * per - 1, 0), 0)),
                  _const_spec(w.shape), _const_spec(b.shape), _const_spec(g.shape), _const_spec(beta.shape)],
        out_specs=pl.BlockSpec((1, ts, c), lambda bi, i: (bi, i, 0)),
        out_shape=jax.ShapeDtypeStruct((bsz, s, c), BF16),
        scratch_shapes=[pltpu.VMEM((ts + CONV_HALO, c), F32)],
        compiler_params=_params("parallel", "parallel"),
        name="conv_prompt",
    )(u, u, w, b, g, beta)


def _conv_sample_kernel(full_ref, w_ref, b_ref, g_ref, beta_ref, o_ref, *, bb, t):
    for bi in range(bb):
        acc = jnp.zeros((t, full_ref.shape[2]), F32)
        for j in range(CONV_WIDTH):
            acc = acc + full_ref[bi, pl.ds(j, t), :] * w_ref[j:j + 1, :]
        y = acc + b_ref[...]
        o_ref[bi] = _ln_swish(y, g_ref[...], beta_ref[...])


def _conv_sample(full, w, b, g, beta, *, bb, t):
    db, rows, c = full.shape
    return pl.pallas_call(
        functools.partial(_conv_sample_kernel, bb=bb, t=t),
        grid=(db // bb,),
        in_specs=[pl.BlockSpec((bb, rows, c), lambda i: (i, 0, 0)),
                  _const_spec(w.shape), _const_spec(b.shape), _const_spec(g.shape), _const_spec(beta.shape)],
        out_specs=pl.BlockSpec((bb, t, c), lambda i: (i, 0, 0)),
        out_shape=jax.ShapeDtypeStruct((db, t, c), F32),
        compiler_params=_params("parallel"),
        name="conv_sample",
    )(full, w, b, g, beta)


def _key_to_score(key):
    return lax.bitcast_convert_type(jnp.where(key < 0, key ^ 0x7FFFFFFF, key), F32)


def _any(flag):
    return jnp.max(jnp.where(flag, 1.0, 0.0)) > 0.0


def _topk_cut(count, smin, smax, n_adm, total, topk, idx_bits):
    kf = float(topk)
    few = n_adm < kf
    lo = jnp.where(few, -jnp.inf, smin)
    c_lo = n_adm

    def probe(lo, hi, c_lo):
        cand = lo * 0.5 + hi * 0.5
        return cand, (cand > lo) & (cand < hi) & (c_lo != kf)

    def cond(st):
        return (st[0] < MAX_VALUE_PASSES) & st[4]

    def body(st):
        p, lo, hi, c_lo, _ = st
        cand, act = probe(lo, hi, c_lo)
        n = count(lambda s, idx: s >= cand)
        up = act & (n >= kf)
        down = act & (n < kf)
        lo, c_lo, hi = jnp.where(up, cand, lo), jnp.where(up, n, c_lo), jnp.where(down, cand, hi)
        return p + 1, lo, hi, c_lo, _any(probe(lo, hi, c_lo)[1])

    st = (jnp.int32(0), lo, smax, c_lo, _any(probe(lo, smax, c_lo)[1]))
    _, lo, _, c_lo, _ = lax.while_loop(cond, body, st)

    n_gt = count(lambda s, idx: s > lo)
    n_ge = jnp.where(few, total, c_lo)
    ok = (n_gt <= kf) & (n_ge >= kf)

    def bit_search():
        zero = jnp.zeros_like(lo)
        tau = jnp.where(count(lambda s, idx: s >= zero) >= kf, 0, INT_MIN).astype(I32)

        def bit_body(b, tau):
            cand = tau | lax.shift_left(jnp.int32(1), jnp.int32(30) - b)
            cand_f = _key_to_score(cand)
            return jnp.where(count(lambda s, idx: s >= cand_f) >= kf, cand, tau)

        tau_f = _key_to_score(lax.fori_loop(0, 31, bit_body, tau))
        tau_f = jnp.where(ok, lo, tau_f)
        return tau_f, count(lambda s, idx: s > tau_f), count(lambda s, idx: s >= tau_f)

    tau, n_gt, n_ge = lax.cond(_any(~ok), bit_search, lambda: (lo, n_gt, n_ge))
    need = kf - n_gt

    def tie_search():
        def body(b, p):
            cand = p + lax.shift_left(jnp.int32(1), jnp.int32(idx_bits - 1) - b)
            n = count(lambda s, idx: (s == tau) & (idx <= cand))
            return jnp.where(n < need, cand, p)
        p = lax.fori_loop(0, idx_bits, body, jnp.full(lo.shape, -1, I32))
        return jnp.where(need > 0.0, p + 1, -1)

    jlim = lax.cond(_any(n_ge > kf), tie_search, lambda: jnp.full(lo.shape, 2 ** idx_bits, I32))
    return tau, jlim


def _selected(score, kidx, tau, jlim):
    return (score > tau) | ((score == tau) & (kidx <= jlim))


def _canonical(score):
    return jnp.where(score == 0.0, 0.0, score)


def _sublane_fold(x, op):
    return op(x.reshape(x.shape[0] // 8, 8, x.shape[1]), axis=0)


def _pattn_kernel(q_ref, qi_ref, wi_ref, ki_ref, k_ref, vt_ref, o_ref, sc_ref, acc_ref,
                  *, tq, topk, n_kv, group, idx_bits):
    i = pl.program_id(1)
    n_chunks = i + 1
    krow = lax.broadcasted_iota(I32, (tq, tq), 0)
    qcol = lax.broadcasted_iota(I32, (tq, tq), 1)

    def index_chunk(c, carry):
        mn, mx, na = carry
        start = pl.multiple_of(c * tq, tq)
        kic = ki_ref[0, pl.ds(start, tq), :]
        acc = jnp.zeros((tq, tq), F32)
        for h in range(N_IDX_HEADS):
            x = lax.dot_general(kic, qi_ref[0, h], NT_DIMS, preferred_element_type=F32)
            acc = acc + wi_ref[0, h:h + 1, :] * jnp.maximum(x, 0.0)
        acc = _canonical(acc)
        future = (c == i) & (krow > qcol)
        sc = jnp.where(future, -jnp.inf, acc)
        sc_ref[c] = sc
        finite = sc > -jnp.inf
        mn = jnp.minimum(mn, _sublane_fold(jnp.where(finite, sc, jnp.inf), jnp.min))
        mx = jnp.maximum(mx, _sublane_fold(sc, jnp.max))
        na = na + _sublane_fold(jnp.where(finite, 1.0, 0.0), jnp.sum)
        return mn, mx, na

    mn, mx, na = lax.fori_loop(0, n_chunks, index_chunk,
                               (jnp.full((8, tq), jnp.inf, F32), jnp.full((8, tq), -jnp.inf, F32),
                                jnp.zeros((8, tq), F32)))
    smin = jnp.min(mn, axis=0, keepdims=True)
    smax = jnp.max(mx, axis=0, keepdims=True)
    n_adm = jnp.sum(na, axis=0, keepdims=True)
    total = (n_chunks * tq).astype(F32)

    def count(pred):
        def body(c, cnt):
            hit = pred(sc_ref[c], c * tq + krow)
            return cnt + _sublane_fold(jnp.where(hit, 1.0, 0.0), jnp.sum)
        cnt = lax.fori_loop(0, n_chunks, body, jnp.zeros((8, tq), F32))
        return jnp.sum(cnt, axis=0, keepdims=True)

    tau, jlim = _topk_cut(count, smin, smax, n_adm, total, topk, idx_bits)

    acc_ref[...] = jnp.zeros(acc_ref.shape, F32)
    gw = group * tq

    def attend_chunk(c, carry):
        start = pl.multiple_of(c * tq, tq)
        kidx = c * tq + krow
        sel = _selected(sc_ref[c], kidx, tau, jlim) & (kidx <= i * tq + qcol)
        bias = jnp.where(sel, 0.0, NEG_BIAS)
        bias = jnp.concatenate([bias] * group, axis=1)
        new = []
        for g in range(n_kv):
            m_old, l_old = carry[g]
            kg = k_ref[0, g, pl.ds(start, tq), :]
            qg = q_ref[0, g * group:(g + 1) * group].reshape(gw, HEAD_DIM)
            s = lax.dot_general(kg, qg, NT_DIMS, preferred_element_type=F32) + bias
            m_new = jnp.maximum(m_old, jnp.max(_sublane_fold(s, jnp.max), axis=0, keepdims=True))
            p = jnp.exp(s - m_new)
            alpha = jnp.exp(m_old - m_new)
            l_new = alpha * l_old + jnp.sum(_sublane_fold(p, jnp.sum), axis=0, keepdims=True)
            pv = jnp.dot(vt_ref[0, c, g * HEAD_DIM:(g + 1) * HEAD_DIM, :], p.astype(BF16),
                         preferred_element_type=F32)
            acc_ref[g] = alpha * acc_ref[g] + pv
            new.append((m_new, l_new))
        return tuple(new)

    init = tuple((jnp.full((1, gw), NEG_BIAS, F32), jnp.zeros((1, gw), F32)) for _ in range(n_kv))
    fin = lax.fori_loop(0, n_chunks, attend_chunk, init)

    outs = []
    for g in range(n_kv):
        og = acc_ref[g] / fin[g][1]
        for r in range(group):
            outs.append(og[:, r * tq:(r + 1) * tq])
    o_ref[0] = jnp.concatenate(outs, axis=0).T.astype(BF16)


def _prompt_attention(q, qi, wi_t, kib, kb, vt, *, tq, topk, n_kv, group):
    bsz, n_heads, s, _ = q.shape
    hk = n_kv * HEAD_DIM
    hq = n_heads * HEAD_DIM
    return pl.pallas_call(
        functools.partial(_pattn_kernel, tq=tq, topk=topk, n_kv=n_kv, group=group,
                          idx_bits=max(s - 1, 1).bit_length()),
        grid=(bsz, s // tq),
        in_specs=[pl.BlockSpec((1, n_heads, tq, HEAD_DIM), lambda bi, i: (bi, 0, i, 0)),
                  pl.BlockSpec((1, N_IDX_HEADS, tq, IDX_DIM), lambda bi, i: (bi, 0, i, 0)),
                  pl.BlockSpec((1, N_IDX_HEADS, tq), lambda bi, i: (bi, 0, i)),
                  pl.BlockSpec((1, s, IDX_DIM), lambda bi, i: (bi, 0, 0)),
                  pl.BlockSpec((1, n_kv, s, HEAD_DIM), lambda bi, i: (bi, 0, 0, 0)),
                  pl.BlockSpec((1, s // tq, hk, tq), lambda bi, i: (bi, 0, 0, 0))],
        out_specs=pl.BlockSpec((1, tq, hq), lambda bi, i: (bi, i, 0)),
        out_shape=jax.ShapeDtypeStruct((bsz, s, hq), BF16),
        scratch_shapes=[pltpu.VMEM((s // tq, tq, tq), F32),
                        pltpu.VMEM((n_kv, HEAD_DIM, group * tq), F32)],
        compiler_params=_params("parallel", "arbitrary"),
        name="prompt_attention",
    )(q, qi, wi_t, kib, kb, vt)


def _lane_fold(m, op):
    out = m[:, :LANES]
    for j in range(1, m.shape[1] // LANES):
        out = op(out, m[:, j * LANES:(j + 1) * LANES])
    return out


def _sattn_kernel(pt_ref, qbd_ref, qi_ref, wi_ref, kin_ref, kn_ref, vn_ref, cidx_ref, ck_ref, cv_ref,
                  o_ref, idx_buf, k_buf, v_buf, sc_ref, scn_ref, out_ref, sem,
                  *, t, kc, n_pages, topk, n_kv, group, idx_bits):
    b = pl.program_id(0)
    past = n_pages * PAGE_SIZE
    n_chunks = past // kc
    n_heads = n_kv * group
    rows_a = n_heads * t

    def page_copies(p):
        page = pt_ref[b, p]
        dst = pl.ds(pl.multiple_of(p * PAGE_SIZE, PAGE_SIZE), PAGE_SIZE)
        return (pltpu.make_async_copy(cidx_ref.at[page], idx_buf.at[dst, :], sem.at[0]),
                pltpu.make_async_copy(ck_ref.at[page], k_buf.at[dst, :], sem.at[1]),
                pltpu.make_async_copy(cv_ref.at[page], v_buf.at[dst, :], sem.at[2]))

    def start_page(p, carry):
        for cp in page_copies(p):
            cp.start()
        return carry

    def wait_page(p, carry):
        for cp in page_copies(p):
            cp.wait()
        return carry

    lax.fori_loop(0, n_pages, start_page, 0)
    lax.fori_loop(0, n_pages, wait_page, 0)

    qi = qi_ref[0]
    wi = wi_ref[0]

    def head_sum(y):
        out = y[0:t]
        for h in range(1, N_IDX_HEADS):
            out = out + y[h * t:(h + 1) * t]
        return out

    def index_chunk(c, carry):
        mn, mx, na = carry
        start = pl.multiple_of(c * kc, kc)
        kic = idx_buf[pl.ds(start, kc), :].astype(BF16)
        x = lax.dot_general(qi, kic, NT_DIMS, preferred_element_type=F32)
        sc = _canonical(head_sum(wi * jnp.maximum(x, 0.0)))
        sc_ref[c] = sc
        finite = sc > -jnp.inf
        return (jnp.minimum(mn, _lane_fold(jnp.where(finite, sc, jnp.inf), jnp.minimum)),
                jnp.maximum(mx, _lane_fold(sc, jnp.maximum)),
                na + _lane_fold(jnp.where(finite, 1.0, 0.0), jnp.add))

    mn, mx, na = lax.fori_loop(0, n_chunks, index_chunk,
                               (jnp.full((t, LANES), jnp.inf, F32), jnp.full((t, LANES), -jnp.inf, F32),
                                jnp.zeros((t, LANES), F32)))

    rown = lax.broadcasted_iota(I32, (t, LANES), 0)
    coln = lax.broadcasted_iota(I32, (t, LANES), 1)
    xn = lax.dot_general(qi, kin_ref[0], NT_DIMS, preferred_element_type=F32)
    scn = _canonical(head_sum(wi * jnp.maximum(xn, 0.0)))
    scn = jnp.where(coln <= rown, scn, -jnp.inf)
    scn_ref[...] = scn
    smin = jnp.min(jnp.minimum(mn, jnp.where(scn > -jnp.inf, scn, jnp.inf)), axis=1, keepdims=True)
    smax = jnp.max(jnp.maximum(mx, scn), axis=1, keepdims=True)
    n_adm = jnp.sum(na + jnp.where(scn > -jnp.inf, 1.0, 0.0), axis=1, keepdims=True)
    total = float(past + LANES)

    colc = lax.broadcasted_iota(I32, (t, kc), 1)

    def count(pred):
        def body(c, cnt):
            hit = pred(sc_ref[c], c * kc + colc)
            return cnt + _lane_fold(jnp.where(hit, 1.0, 0.0), jnp.add)
        cnt = lax.fori_loop(0, n_chunks, body, jnp.zeros((t, LANES), F32))
        cnt = cnt + jnp.where(pred(scn_ref[...], past + coln), 1.0, 0.0)
        return jnp.sum(cnt, axis=1, keepdims=True)

    tau, jlim = _topk_cut(count, smin, smax, n_adm, total, topk, idx_bits)

    def bias_of(sc, kpos):
        return jnp.where(_selected(sc, kpos, tau, jlim), 0.0, NEG_BIAS)

    qbd = qbd_ref[0]

    def softmax_step(state, s, vals):
        m_old, l_old, acc = state
        m_new = jnp.maximum(m_old, jnp.max(s, axis=1, keepdims=True))
        p = jnp.exp(s - m_new)
        alpha = jnp.exp(m_old - m_new)
        l_new = alpha * l_old + jnp.sum(p, axis=1, keepdims=True)
        acc = alpha * acc + jnp.dot(p.astype(BF16), vals, preferred_element_type=F32)
        return m_new, l_new, acc

    def attend_chunk(c, state):
        start = pl.multiple_of(c * kc, kc)
        keys = k_buf[pl.ds(start, kc), :].astype(BF16)
        vals = v_buf[pl.ds(start, kc), :].astype(BF16)
        bias = jnp.concatenate([bias_of(sc_ref[c], c * kc + colc)] * n_heads, axis=0)
        s = lax.dot_general(qbd, keys, NT_DIMS, preferred_element_type=F32) + bias
        return softmax_step(state, s, vals)

    hk = n_kv * HEAD_DIM
    state = (jnp.full((rows_a, 1), NEG_BIAS, F32), jnp.zeros((rows_a, 1), F32), jnp.zeros((rows_a, hk), F32))
    state = lax.fori_loop(0, n_chunks, attend_chunk, state)
    biasn = jnp.where(coln <= rown, bias_of(scn_ref[...], past + coln), NEG_BIAS)
    sn = lax.dot_general(qbd, kn_ref[0], NT_DIMS, preferred_element_type=F32)
    sn = sn + jnp.concatenate([biasn] * n_heads, axis=0)
    _, l_fin, acc = softmax_step(state, sn, vn_ref[0])
    o = acc / l_fin
    for h in range(n_heads):
        g = h // group
        out_ref[:, h * HEAD_DIM:(h + 1) * HEAD_DIM] = o[h * t:(h + 1) * t, g * HEAD_DIM:(g + 1) * HEAD_DIM]
    o_ref[0] = out_ref[...]


def _sample_attention(page_table, qbd, qi, wi, ki_new, k_new, v_new, cache_idx, cache_k, cache_v,
                      *, t, kc, topk, n_kv, group):
    db, n_pages = page_table.shape
    past = n_pages * PAGE_SIZE
    hk = n_kv * HEAD_DIM
    hq = n_kv * group * HEAD_DIM
    blk = lambda b, pt: (b, 0, 0)
    any_spec = pl.BlockSpec(memory_space=pl.ANY)
    grid_spec = pltpu.PrefetchScalarGridSpec(
        num_scalar_prefetch=1,
        grid=(db,),
        in_specs=[pl.BlockSpec((1,) + qbd.shape[1:], blk), pl.BlockSpec((1,) + qi.shape[1:], blk),
                  pl.BlockSpec((1,) + wi.shape[1:], blk), pl.BlockSpec((1,) + ki_new.shape[1:], blk),
                  pl.BlockSpec((1,) + k_new.shape[1:], blk), pl.BlockSpec((1,) + v_new.shape[1:], blk),
                  any_spec, any_spec, any_spec],
        out_specs=pl.BlockSpec((1, t, hq), blk),
        scratch_shapes=[pltpu.VMEM((past, IDX_DIM), F32), pltpu.VMEM((past, hk), F32), pltpu.VMEM((past, hk), F32),
                        pltpu.VMEM((past // kc, t, kc), F32), pltpu.VMEM((t, LANES), F32),
                        pltpu.VMEM((t, hq), F32), pltpu.SemaphoreType.DMA((3,))],
    )
    return pl.pallas_call(
        functools.partial(_sattn_kernel, t=t, kc=kc, n_pages=n_pages, topk=topk, n_kv=n_kv, group=group,
                          idx_bits=max(past + LANES - 1, 1).bit_length()),
        grid_spec=grid_spec,
        out_shape=jax.ShapeDtypeStruct((db, t, hq), F32),
        compiler_params=_params("arbitrary"),
        name="sample_attention",
    )(page_table, qbd, qi, wi, ki_new, k_new, v_new, cache_idx, cache_k, cache_v)


def _merge_ffn_kernel(x_ref, conv_ref, attn_ref, wo_ref, g_ref, wg_ref, wu_ref, wd_ref, o_ref, *, c_conv):
    conv = conv_ref[...].astype(BF16)
    attn = attn_ref[...].astype(BF16)
    x = (x_ref[...] + jnp.dot(conv, wo_ref[:c_conv, :], preferred_element_type=F32)
         + jnp.dot(attn, wo_ref[c_conv:, :], preferred_element_type=F32))
    ms = jnp.mean(x * x, axis=-1, keepdims=True)
    h = ((x * lax.rsqrt(ms + 1e-6)) * g_ref[...]).astype(BF16)
    ffn = None
    for ci in range(wg_ref.shape[0]):
        gate = jnp.dot(h, wg_ref[ci], preferred_element_type=F32)
        up = jnp.dot(h, wu_ref[ci], preferred_element_type=F32)
        f = ((gate * jax.nn.sigmoid(gate)) * up).astype(BF16)
        part = jnp.dot(f, wd_ref[ci], preferred_element_type=F32)
        ffn = part if ffn is None else ffn + part
    o_ref[...] = x + ffn


def _merge_ffn(x2d, conv, attn, wo, g, wg, wu, wd, *, tm):
    t, d = x2d.shape
    c_conv = conv.shape[1]
    row = lambda i: (i, 0)
    return pl.pallas_call(
        functools.partial(_merge_ffn_kernel, c_conv=c_conv),
        grid=(t // tm,),
        in_specs=[pl.BlockSpec((tm, d), row), pl.BlockSpec((tm, c_conv), row), pl.BlockSpec((tm, attn.shape[1]), row),
                  _const_spec(wo.shape), _const_spec(g.shape), _const_spec(wg.shape), _const_spec(wu.shape),
                  _const_spec(wd.shape)],
        out_specs=pl.BlockSpec((tm, d), row),
        out_shape=jax.ShapeDtypeStruct((t, d), F32),
        compiler_params=_params("parallel"),
        name="merge_ffn",
    )(x2d, conv, attn, wo, g, wg, wu, wd)


def _ffn_chunks(d_ff):
    for n in (2, 3, 4):
        if d_ff % (n * LANES) == 0:
            return n
    return 1


def kernel(x_prompt, x_sample, cache_k, cache_v, cache_idx_k, state_conv, page_table, attn_norm_g, w_in,
           q_norm_g, k_norm_g, conv_dw_w, conv_dw_b, conv_ln_g, conv_ln_b, w_out, ffn_norm_g, w_gate, w_up,
           w_down):
    bsz, s, d = x_prompt.shape
    db, t, _ = x_sample.shape
    depth = w_in.shape[0]
    assert depth == 1, "single-layer step"
    n_pool = cache_k.shape[1]
    n_kv = cache_k.shape[3]
    c_conv = state_conv.shape[3]
    hk = n_kv * HEAD_DIM
    hq = w_out.shape[1] - c_conv
    group = hq // hk
    hi = N_IDX_HEADS * IDX_DIM
    d_main = 2 * c_conv + hq + 2 * hk + hi
    d_ff = w_gate.shape[2]
    past = page_table.shape[1] * PAGE_SIZE
    assert w_in.shape[2] == d_main + IDX_DIM + N_IDX_HEADS
    assert c_conv % LANES == 0 and hq % LANES == 0 and hk % LANES == 0
    topk_p = min(TOPK_MAX, s // 4)
    topk_s = min(TOPK_MAX, (past + t) // 4)

    tm = min(512, s)
    tq = min(256, s)
    assert s % tm == 0 and s % tq == 0 and (db * t) % min(tm, db * t) == 0 and tq >= topk_p

    w_tail = jnp.pad(w_in[0, :, d_main:], ((0, 0), (0, LANES - IDX_DIM - N_IDX_HEADS)))
    w_packed = jnp.concatenate([w_in[0, :, :d_main], w_tail], axis=1).astype(BF16)
    qg = jnp.tile(q_norm_g, (1, hq // HEAD_DIM))
    kg = jnp.tile(k_norm_g, (1, hk // HEAD_DIM))
    head_of = jnp.arange(hq) // HEAD_DIM
    bd = (head_of[:, None] == head_of[None, :]).astype(BF16)
    wo = w_out[0].astype(BF16)
    n_fc = _ffn_chunks(d_ff)
    col_chunks = lambda w: w.reshape(d, n_fc, d_ff // n_fc).transpose(1, 0, 2).astype(BF16)
    wg, wu = col_chunks(w_gate[0]), col_chunks(w_up[0])
    wd = w_down[0].reshape(n_fc, d_ff // n_fc, d).astype(BF16)
    dims = dict(c_conv=c_conv, hq=hq, hk=hk, hi=hi)

    xp2 = x_prompt.reshape(bsz * s, d)
    tabs_p = _rope_tables(jnp.arange(s, dtype=I32))
    u, q, k, kb, v, vb, qi, ki, kib, wi = _project(xp2, tabs_p, attn_norm_g, w_packed, qg, kg, bd, tm=tm, **dims)
    u3 = u.reshape(bsz, s, c_conv)
    conv_p = _conv_prompt(u3, conv_dw_w[0], conv_dw_b, conv_ln_g, conv_ln_b, ts=min(256, s), rt=64)
    n_heads = hq // HEAD_DIM
    heads_major = lambda a, n: a.reshape(bsz, s, n, HEAD_DIM).transpose(0, 2, 1, 3)
    vt = vb.reshape(bsz, s // tq, tq, hk).transpose(0, 1, 3, 2)
    attn_p = _prompt_attention(heads_major(q, n_heads), heads_major(qi, N_IDX_HEADS),
                               wi.reshape(bsz, s, N_IDX_HEADS).transpose(0, 2, 1),
                               kib.reshape(bsz, s, IDX_DIM), heads_major(kb, n_kv), vt,
                               tq=tq, topk=topk_p, n_kv=n_kv, group=group)
    y_p = _merge_ffn(xp2, conv_p.reshape(bsz * s, c_conv), attn_p.reshape(bsz * s, hq), wo, ffn_norm_g,
                     wg, wu, wd, tm=tm)

    ts_tok = db * t
    tms = min(tm, ts_tok)
    xs2 = x_sample.reshape(ts_tok, d)
    pos_s = past + (jnp.arange(tms, dtype=I32) % t)
    tabs_s = _rope_tables(pos_s)
    us, qs, ks, ksb, vs, vsb, qis, kis, kisb, wis = _project(xs2, tabs_s, attn_norm_g, w_packed, qg, kg, bd,
                                                             tm=tms, **dims)
    full_s = jnp.concatenate([state_conv[0], us.reshape(db, t, c_conv)], axis=1)
    conv_s = _conv_sample(full_s, conv_dw_w[0], conv_dw_b, conv_ln_g, conv_ln_b, bb=min(8, db), t=t)

    q_ht = qs.reshape(db, t, n_heads, HEAD_DIM).transpose(0, 2, 1, 3)
    kv_of = jnp.arange(n_heads) // group
    onehot = (kv_of[:, None] == jnp.arange(n_kv)[None, :]).astype(BF16)
    qbd = (q_ht[:, :, :, None, :] * onehot[None, :, None, :, None]).reshape(db, n_heads * t, hk)
    qi_ht = qis.reshape(db, t, N_IDX_HEADS, IDX_DIM).transpose(0, 2, 1, 3).reshape(db, N_IDX_HEADS * t, IDX_DIM)
    wi_ht = wis.reshape(db, t, N_IDX_HEADS).transpose(0, 2, 1).reshape(db, N_IDX_HEADS * t, 1)
    pad_rows = lambda a: jnp.pad(a.reshape(db, t, a.shape[-1]), ((0, 0), (0, LANES - t), (0, 0)))
    attn_s = _sample_attention(page_table, qbd, qi_ht, wi_ht, pad_rows(kisb), pad_rows(ksb), pad_rows(vsb),
                               cache_idx_k[0], cache_k[0].reshape(n_pool, PAGE_SIZE, hk),
                               cache_v[0].reshape(n_pool, PAGE_SIZE, hk),
                               t=t, kc=min(512, past), topk=topk_s, n_kv=n_kv, group=group)
    y_s = _merge_ffn(xs2, conv_s.reshape(ts_tok, c_conv), attn_s.reshape(ts_tok, hq), wo, ffn_norm_g,
                     wg, wu, wd, tm=tms)

    return (y_p.reshape(bsz, s, d), y_s.reshape(db, t, d),
            k.reshape(1, bsz, s, n_kv, HEAD_DIM), v.reshape(1, bsz, s, n_kv, HEAD_DIM),
            ki.reshape(1, bsz, s, IDX_DIM), u3[:, s - (CONV_WIDTH - 1):][None],
            ks.reshape(1, db, t, n_kv, HEAD_DIM), vs.reshape(1, db, t, n_kv, HEAD_DIM),
            kis.reshape(1, db, t, IDX_DIM), full_s[:, t:][None])
```

```python
import functools

import jax
import jax.numpy as jnp
from jax import lax
from jax.experimental import pallas as pl
from jax.experimental.pallas import tpu as pltpu

F32 = jnp.float32
BF16 = jnp.bfloat16
I32 = jnp.int32

HEAD_DIM = 64
IDX_DIM = 64
N_IDX_HEADS = 16
TOPK_MAX = 256
CONV_WIDTH = 31
ROPE_THETA = 500000.0
ROPE_FRACTION = 4
ROPE_HALF = HEAD_DIM // ROPE_FRACTION // 2
PAGE_SIZE = 128
LANES = 128
CONV_HALO = 32
INT_MIN = -(2 ** 31)
NEG_BIAS = -1e30
MAX_VALUE_PASSES = 48
VMEM_LIMIT = 56 * 1024 * 1024
NT_DIMS = (((1,), (1,)), ((), ()))


def _params(*sem):
    return pltpu.CompilerParams(dimension_semantics=sem, vmem_limit_bytes=VMEM_LIMIT)


def _const_spec(shape):
    zeros = (0,) * len(shape)
    return pl.BlockSpec(shape, lambda *_: zeros)


def _rope_tables(pos):
    ang = pos.astype(F32)[:, None] * (ROPE_THETA ** (-jnp.arange(ROPE_HALF, dtype=F32) / ROPE_HALF))[None, :]
    cos, sin = jnp.cos(ang), jnp.sin(ang)
    t = pos.shape[0]
    rest = HEAD_DIM - 2 * ROPE_HALF
    c = jnp.concatenate([cos, cos, jnp.ones((t, rest), F32)], axis=1)
    a = jnp.concatenate([-sin, jnp.zeros((t, HEAD_DIM - ROPE_HALF), F32)], axis=1)
    b = jnp.concatenate([jnp.zeros((t, ROPE_HALF), F32), sin, jnp.zeros((t, rest), F32)], axis=1)
    rep = LANES // HEAD_DIM
    return tuple(jnp.tile(m, (1, rep)) for m in (c, a, b))


def _proj_kernel(x_ref, g_ref, w_ref, qg_ref, kg_ref, bd_ref, rc_ref, ra_ref, rb_ref,
                 u_ref, q_ref, k_ref, kb_ref, v_ref, vb_ref, qi_ref, ki_ref, kib_ref, wi_ref,
                 *, c_conv, hq, hk, hi):
    x = x_ref[...]
    ms = jnp.mean(x * x, axis=-1, keepdims=True)
    h = ((x * lax.rsqrt(ms + 1e-6)) * g_ref[...]).astype(BF16)
    rc, ra, rb = rc_ref[...], ra_ref[...], rb_ref[...]

    def seg(off, width):
        return jnp.dot(h, w_ref[:, off:off + width], preferred_element_type=F32)

    def rope(z):
        parts = []
        for j in range(z.shape[1] // LANES):
            zj = z[:, j * LANES:(j + 1) * LANES]
            parts.append(zj * rc + pltpu.roll(zj, LANES - ROPE_HALF, 1) * ra + pltpu.roll(zj, ROPE_HALF, 1) * rb)
        return parts[0] if len(parts) == 1 else jnp.concatenate(parts, axis=1)

    def head_norm(z, gain_ref):
        w = z.shape[1]
        s = z * z
        s_hi = s.astype(BF16)
        s_lo = (s - s_hi.astype(F32)).astype(BF16)
        bd = bd_ref[:w, :w]
        ssq = jnp.dot(s_hi, bd, preferred_element_type=F32) + jnp.dot(s_lo, bd, preferred_element_type=F32)
        return (z * lax.rsqrt(ssq * (1.0 / HEAD_DIM) + 1e-6)) * gain_ref[:, :w]

    off = 0
    u_val = seg(off, c_conv)
    off += c_conv
    u_gate = seg(off, c_conv)
    off += c_conv
    u_ref[...] = u_val * jax.nn.sigmoid(u_gate)

    q = rope(head_norm(seg(off, hq), qg_ref))
    off += hq
    q_ref[...] = (q * (HEAD_DIM ** -0.5)).astype(BF16)

    k = rope(head_norm(seg(off, hk), kg_ref))
    off += hk
    k_ref[...] = k
    kb_ref[...] = k.astype(BF16)

    v = seg(off, hk)
    off += hk
    v_ref[...] = v
    vb_ref[...] = v.astype(BF16)

    qi_ref[...] = rope(seg(off, hi)).astype(BF16)
    off += hi

    tail = seg(off, LANES)
    ki = rope(tail)[:, :IDX_DIM]
    ki_ref[...] = ki
    kib_ref[...] = ki.astype(BF16)
    wi_ref[...] = tail[:, IDX_DIM:IDX_DIM + N_IDX_HEADS] * ((N_IDX_HEADS * IDX_DIM) ** -0.5)


def _project(x2d, tabs, g, w_packed, qg, kg, bd, *, tm, c_conv, hq, hk, hi):
    t, d = x2d.shape
    ntab = tabs[0].shape[0] // tm
    row = lambda i: (i, 0)
    tab_spec = pl.BlockSpec((tm, LANES), lambda i: (i % ntab, 0))
    outs = [
        (c_conv, F32), (hq, BF16), (hk, F32), (hk, BF16), (hk, F32), (hk, BF16),
        (hi, BF16), (IDX_DIM, F32), (IDX_DIM, BF16), (N_IDX_HEADS, F32),
    ]
    return pl.pallas_call(
        functools.partial(_proj_kernel, c_conv=c_conv, hq=hq, hk=hk, hi=hi),
        grid=(t // tm,),
        in_specs=[pl.BlockSpec((tm, d), row), _const_spec(g.shape), _const_spec(w_packed.shape),
                  _const_spec(qg.shape), _const_spec(kg.shape), _const_spec(bd.shape),
                  tab_spec, tab_spec, tab_spec],
        out_specs=[pl.BlockSpec((tm, w), row) for w, _ in outs],
        out_shape=[jax.ShapeDtypeStruct((t, w), dt) for w, dt in outs],
        compiler_params=_params("parallel"),
        name="project",
    )(x2d, g, w_packed, qg, kg, bd, *tabs)


def _ln_swish(y, g, b):
    mu = jnp.mean(y, axis=-1, keepdims=True)
    yc = y - mu
    var = jnp.mean(yc * yc, axis=-1, keepdims=True)
    yn = (yc * lax.rsqrt(var + 1e-5)) * g + b
    return yn * jax.nn.sigmoid(yn)


def _conv_prompt_kernel(u_ref, up_ref, w_ref, b_ref, g_ref, beta_ref, o_ref, full_ref, *, ts, rt):
    i = pl.program_id(1)
    full_ref[0:CONV_HALO, :] = jnp.where(i > 0, up_ref[0], 0.0)
    full_ref[CONV_HALO:, :] = u_ref[0]
    base = CONV_HALO - (CONV_WIDTH - 1)
    for r0 in range(0, ts, rt):
        acc = jnp.zeros((rt, u_ref.shape[2]), F32)
        for j in range(CONV_WIDTH):
            acc = acc + full_ref[pl.ds(r0 + base + j, rt), :] * w_ref[j:j + 1, :]
        y = acc + b_ref[...]
        o_ref[0, r0:r0 + rt, :] = _ln_swish(y, g_ref[...], beta_ref[...]).astype(BF16)


def _conv_prompt(u, w, b, g, beta, *, ts, rt):
    bsz, s, c = u.shape
    per = ts // CONV_HALO
    return pl.pallas_call(
        functools.partial(_conv_prompt_kernel, ts=ts, rt=rt),
        grid=(bsz, s // ts),
        in_specs=[pl.BlockSpec((1, ts, c), lambda bi, i: (bi, i, 0)),
                  pl.BlockSpec((1, CONV_HALO, c), lambda bi, i: (bi, jnp.maximum(i * per - 1, 0), 0)),
                  _const_spec(w.shape), _const_spec(b.shape), _const_spec(g.shape), _const_spec(beta.shape)],
        out_specs=pl.BlockSpec((1, ts, c), lambda bi, i: (bi, i, 0)),
        out_shape=jax.ShapeDtypeStruct((bsz, s, c), BF16),
        scratch_shapes=[pltpu.VMEM((ts + CONV_HALO, c), F32)],
        compiler_params=_params("parallel", "parallel"),
        name="conv_prompt",
    )(u, u, w, b, g, beta)


def _conv_sample_kernel(full_ref, w_ref, b_ref, g_ref, beta_ref, o_ref, *, bb, t):
    for bi in range(bb):
        acc = jnp.zeros((t, full_ref.shape[2]), F32)
        for j in range(CONV_WIDTH):
            acc = acc + full_ref[bi, pl.ds(j, t), :] * w_ref[j:j + 1, :]
        y = acc + b_ref[...]
        o_ref[bi] = _ln_swish(y, g_ref[...], beta_ref[...])


def _conv_sample(full, w, b, g, beta, *, bb, t):
    db, rows, c = full.shape
    return pl.pallas_call(
        functools.partial(_conv_sample_kernel, bb=bb, t=t),
        grid=(db // bb,),
        in_specs=[pl.BlockSpec((bb, rows, c), lambda i: (i, 0, 0)),
                  _const_spec(w.shape), _const_spec(b.shape), _const_spec(g.shape), _const_spec(beta.shape)],
        out_specs=pl.BlockSpec((bb, t, c), lambda i: (i, 0, 0)),
        out_shape=jax.ShapeDtypeStruct((db, t, c), F32),
        compiler_params=_params("parallel"),
        name="conv_sample",
    )(full, w, b, g, beta)


def _key_to_score(key):
    return lax.bitcast_convert_type(jnp.where(key < 0, key ^ 0x7FFFFFFF, key), F32)


def _any(flag):
    return jnp.max(jnp.where(flag, 1.0, 0.0)) > 0.0


def _topk_cut(count, smin, smax, n_adm, total, topk, idx_bits):
    kf = float(topk)
    few = n_adm < kf
    lo = jnp.where(few, -jnp.inf, smin)
    c_lo = n_adm

    def probe(lo, hi, c_lo):
        cand = lo * 0.5 + hi * 0.5
        return cand, (cand > lo) & (cand < hi) & (c_lo != kf)

    def cond(st):
        return (st[0] < MAX_VALUE_PASSES) & st[4]

    def body(st):
        p, lo, hi, c_lo, _ = st
        cand, act = probe(lo, hi, c_lo)
        n = count(lambda s, idx: s >= cand)
        up = act & (n >= kf)
        down = act & (n < kf)
        lo, c_lo, hi = jnp.where(up, cand, lo), jnp.where(up, n, c_lo), jnp.where(down, cand, hi)
        return p + 1, lo, hi, c_lo, _any(probe(lo, hi, c_lo)[1])

    st = (jnp.int32(0), lo, smax, c_lo, _any(probe(lo, smax, c_lo)[1]))
    _, lo, _, c_lo, _ = lax.while_loop(cond, body, st)

    n_gt = count(lambda s, idx: s > lo)
    n_ge = jnp.where(few, total, c_lo)
    ok = (n_gt <= kf) & (n_ge >= kf)

    def bit_search():
        zero = jnp.zeros_like(lo)
        tau = jnp.where(count(lambda s, idx: s >= zero) >= kf, 0, INT_MIN).astype(I32)

        def bit_body(b, tau):
            cand = tau | lax.shift_left(jnp.int32(1), jnp.int32(30) - b)
            cand_f = _key_to_score(cand)
            return jnp.where(count(lambda s, idx: s >= cand_f) >= kf, cand, tau)

        tau_f = _key_to_score(lax.fori_loop(0, 31, bit_body, tau))
        tau_f = jnp.where(ok, lo, tau_f)
        return tau_f, count(lambda s, idx: s > tau_f), count(lambda s, idx: s >= tau_f)

    tau, n_gt, n_ge = lax.cond(_any(~ok), bit_search, lambda: (lo, n_gt, n_ge))
    need = kf - n_gt

    def tie_search():
        def body(b, p):
            cand = p + lax.shift_left(jnp.int32(1), jnp.int32(idx_bits - 1) - b)
            n = count(lambda s, idx: (s == tau) & (idx <= cand))
            return jnp.where(n < need, cand, p)
        p = lax.fori_loop(0, idx_bits, body, jnp.full(lo.shape, -1, I32))
        return jnp.where(need > 0.0, p + 1, -1)

    jlim = lax.cond(_any(n_ge > kf), tie_search, lambda: jnp.full(lo.shape, 2 ** idx_bits, I32))
    return tau, jlim


def _selected(score, kidx, tau, jlim):
    return (score > tau) | ((score == tau) & (kidx <= jlim))


def _canonical(score):
    return jnp.where(score == 0.0, 0.0, score)


def _sublane_fold(x, op):
    return op(x.reshape(x.shape[0] // 8, 8, x.shape[1]), axis=0)


def _pattn_kernel(q_ref, qi_ref, wi_ref, ki_ref, k_ref, vt_ref, o_ref, sc_ref, acc_ref,
                  *, tq, topk, n_kv, group, idx_bits):
    i = pl.program_id(1)
    n_chunks = i + 1
    krow = lax.broadcasted_iota(I32, (tq, tq), 0)
    qcol = lax.broadcasted_iota(I32, (tq, tq), 1)

    def index_chunk(c, carry):
        mn, mx, na = carry
        start = pl.multiple_of(c * tq, tq)
        kic = ki_ref[0, pl.ds(start, tq), :]
        acc = jnp.zeros((tq, tq), F32)
        for h in range(N_IDX_HEADS):
            x = lax.dot_general(kic, qi_ref[0, h], NT_DIMS, preferred_element_type=F32)
            acc = acc + wi_ref[0, h:h + 1, :] * jnp.maximum(x, 0.0)
        acc = _canonical(acc)
        future = (c == i) & (krow > qcol)
        sc = jnp.where(future, -jnp.inf, acc)
        sc_ref[c] = sc
        finite = sc > -jnp.inf
        mn = jnp.minimum(mn, _sublane_fold(jnp.where(finite, sc, jnp.inf), jnp.min))
        mx = jnp.maximum(mx, _sublane_fold(sc, jnp.max))
        na = na + _sublane_fold(jnp.where(finite, 1.0, 0.0), jnp.sum)
        return mn, mx, na

    mn, mx, na = lax.fori_loop(0, n_chunks, index_chunk,
                               (jnp.full((8, tq), jnp.inf, F32), jnp.full((8, tq), -jnp.inf, F32),
                                jnp.zeros((8, tq), F32)))
    smin = jnp.min(mn, axis=0, keepdims=True)
    smax = jnp.max(mx, axis=0, keepdims=True)
    n_adm = jnp.sum(na, axis=0, keepdims=True)
    total = (n_chunks * tq).astype(F32)

    def count(pred):
        def body(c, cnt):
            hit = pred(sc_ref[c], c * tq + krow)
            return cnt + _sublane_fold(jnp.where(hit, 1.0, 0.0), jnp.sum)
        cnt = lax.fori_loop(0, n_chunks, body, jnp.zeros((8, tq), F32))
        return jnp.sum(cnt, axis=0, keepdims=True)

    tau, jlim = _topk_cut(count, smin, smax, n_adm, total, topk, idx_bits)

    acc_ref[...] = jnp.zeros(acc_ref.shape, F32)
    gw = group * tq

    def attend_chunk(c, carry):
        start = pl.multiple_of(c * tq, tq)
        kidx = c * tq + krow
        sel = _selected(sc_ref[c], kidx, tau, jlim) & (kidx <= i * tq + qcol)
        bias = jnp.where(sel, 0.0, NEG_BIAS)
        bias = jnp.concatenate([bias] * group, axis=1)
        new = []
        for g in range(n_kv):
            m_old, l_old = carry[g]
            kg = k_ref[0, g, pl.ds(start, tq), :]
            qg = q_ref[0, g * group:(g + 1) * group].reshape(gw, HEAD_DIM)
            s = lax.dot_general(kg, qg, NT_DIMS, preferred_element_type=F32) + bias
            m_new = jnp.maximum(m_old, jnp.max(_sublane_fold(s, jnp.max), axis=0, keepdims=True))
            p = jnp.exp(s - m_new)
            alpha = jnp.exp(m_old - m_new)
            l_new = alpha * l_old + jnp.sum(_sublane_fold(p, jnp.sum), axis=0, keepdims=True)
            pv = jnp.dot(vt_ref[0, c, g * HEAD_DIM:(g + 1) * HEAD_DIM, :], p.astype(BF16),
                         preferred_element_type=F32)
            acc_ref[g] = alpha * acc_ref[g] + pv
            new.append((m_new, l_new))
        return tuple(new)

    init = tuple((jnp.full((1, gw), NEG_BIAS, F32), jnp.zeros((1, gw), F32)) for _ in range(n_kv))
    fin = lax.fori_loop(0, n_chunks, attend_chunk, init)

    outs = []
    for g in range(n_kv):
        og = acc_ref[g] / fin[g][1]
        for r in range(group):
            outs.append(og[:, r * tq:(r + 1) * tq])
    o_ref[0] = jnp.concatenate(outs, axis=0).T.astype(BF16)


def _prompt_attention(q, qi, wi_t, kib, kb, vt, *, tq, topk, n_kv, group):
    bsz, n_heads, s, _ = q.shape
    hk = n_kv * HEAD_DIM
    hq = n_heads * HEAD_DIM
    return pl.pallas_call(
        functools.partial(_pattn_kernel, tq=tq, topk=topk, n_kv=n_kv, group=group,
                          idx_bits=max(s - 1, 1).bit_length()),
        grid=(bsz, s // tq),
        in_specs=[pl.BlockSpec((1, n_heads, tq, HEAD_DIM), lambda bi, i: (bi, 0, i, 0)),
                  pl.BlockSpec((1, N_IDX_HEADS, tq, IDX_DIM), lambda bi, i: (bi, 0, i, 0)),
                  pl.BlockSpec((1, N_IDX_HEADS, tq), lambda bi, i: (bi, 0, i)),
                  pl.BlockSpec((1, s, IDX_DIM), lambda bi, i: (bi, 0, 0)),
                  pl.BlockSpec((1, n_kv, s, HEAD_DIM), lambda bi, i: (bi, 0, 0, 0)),
                  pl.BlockSpec((1, s // tq, hk, tq), lambda bi, i: (bi, 0, 0, 0))],
        out_specs=pl.BlockSpec((1, tq, hq), lambda bi, i: (bi, i, 0)),
        out_shape=jax.ShapeDtypeStruct((bsz, s, hq), BF16),
        scratch_shapes=[pltpu.VMEM((s // tq, tq, tq), F32),
                        pltpu.VMEM((n_kv, HEAD_DIM, group * tq), F32)],
        compiler_params=_params("parallel", "arbitrary"),
        name="prompt_attention",
    )(q, qi, wi_t, kib, kb, vt)


def _lane_fold(m, op):
    out = m[:, :LANES]
    for j in range(1, m.shape[1] // LANES):
        out = op(out, m[:, j * LANES:(j + 1) * LANES])
    return out


def _sattn_kernel(pt_ref, qbd_ref, qi_ref, wi_ref, kin_ref, kn_ref, vn_ref, cidx_ref, ck_ref, cv_ref,
                  o_ref, idx_buf, k_buf, v_buf, sc_ref, scn_ref, out_ref, sem,
                  *, t, kc, n_pages, topk, n_kv, group, idx_bits):
    b = pl.program_id(0)
    past = n_pages * PAGE_SIZE
    n_chunks = past // kc
    ppc = kc // PAGE_SIZE
    n_heads = n_kv * group
    rows_a = n_heads * t
    slot = b % 2

    def page_copies(batch, sl, p):
        page = pt_ref[batch, p]
        return (pltpu.make_async_copy(cidx_ref.at[page], idx_buf.at[sl, p], sem.at[sl, 0]),
                pltpu.make_async_copy(ck_ref.at[page], k_buf.at[sl, p], sem.at[sl, 1]),
                pltpu.make_async_copy(cv_ref.at[page], v_buf.at[sl, p], sem.at[sl, 2]))

    def start_batch(batch, sl):
        def body(p, carry):
            for cp in page_copies(batch, sl, p):
                cp.start()
            return carry
        lax.fori_loop(0, n_pages, body, 0)

    @pl.when(b == 0)
    def _():
        start_batch(0, 0)

    @pl.when(b + 1 < pl.num_programs(0))
    def _():
        start_batch(b + 1, 1 - slot)

    def wait_page(p, carry):
        for cp in page_copies(b, slot, p):
            cp.wait()
        return carry

    lax.fori_loop(0, n_pages, wait_page, 0)

    def chunk_of(buf, c):
        return jnp.concatenate([buf[slot, c * ppc + j] for j in range(ppc)], axis=1).astype(BF16)

    qi = qi_ref[0]
    wi = wi_ref[0]

    def head_sum(y):
        out = y[0:t]
        for h in range(1, N_IDX_HEADS):
            out = out + y[h * t:(h + 1) * t]
        return out

    def index_chunk(c, carry):
        mn, mx, na = carry
        x = jnp.dot(qi, chunk_of(idx_buf, c), preferred_element_type=F32)
        sc = _canonical(head_sum(wi * jnp.maximum(x, 0.0)))
        sc_ref[c] = sc
        finite = sc > -jnp.inf
        return (jnp.minimum(mn, _lane_fold(jnp.where(finite, sc, jnp.inf), jnp.minimum)),
                jnp.maximum(mx, _lane_fold(sc, jnp.maximum)),
                na + _lane_fold(jnp.where(finite, 1.0, 0.0), jnp.add))

    mn, mx, na = lax.fori_loop(0, n_chunks, index_chunk,
                               (jnp.full((t, LANES), jnp.inf, F32), jnp.full((t, LANES), -jnp.inf, F32),
                                jnp.zeros((t, LANES), F32)))

    rown = lax.broadcasted_iota(I32, (t, LANES), 0)
    coln = lax.broadcasted_iota(I32, (t, LANES), 1)
    xn = lax.dot_general(qi, kin_ref[0], NT_DIMS, preferred_element_type=F32)
    scn = _canonical(head_sum(wi * jnp.maximum(xn, 0.0)))
    scn = jnp.where(coln <= rown, scn, -jnp.inf)
    scn_ref[...] = scn
    smin = jnp.min(jnp.minimum(mn, jnp.where(scn > -jnp.inf, scn, jnp.inf)), axis=1, keepdims=True)
    smax = jnp.max(jnp.maximum(mx, scn), axis=1, keepdims=True)
    n_adm = jnp.sum(na + jnp.where(scn > -jnp.inf, 1.0, 0.0), axis=1, keepdims=True)
    total = float(past + LANES)

    colc = lax.broadcasted_iota(I32, (t, kc), 1)

    def count(pred):
        def body(c, cnt):
            hit = pred(sc_ref[c], c * kc + colc)
            return cnt + _lane_fold(jnp.where(hit, 1.0, 0.0), jnp.add)
        cnt = lax.fori_loop(0, n_chunks, body, jnp.zeros((t, LANES), F32))
        cnt = cnt + jnp.where(pred(scn_ref[...], past + coln), 1.0, 0.0)
        return jnp.sum(cnt, axis=1, keepdims=True)

    tau, jlim = _topk_cut(count, smin, smax, n_adm, total, topk, idx_bits)

    def bias_of(sc, kpos):
        return jnp.where(_selected(sc, kpos, tau, jlim), 0.0, NEG_BIAS)

    qbd = qbd_ref[0]

    def softmax_step(state, s, weigh):
        m_old, l_old, acc = state
        m_new = jnp.maximum(m_old, jnp.max(s, axis=1, keepdims=True))
        p = jnp.exp(s - m_new)
        alpha = jnp.exp(m_old - m_new)
        l_new = alpha * l_old + jnp.sum(p, axis=1, keepdims=True)
        return m_new, l_new, alpha * acc + weigh(p.astype(BF16))

    def attend_chunk(c, state):
        bias = jnp.concatenate([bias_of(sc_ref[c], c * kc + colc)] * n_heads, axis=0)
        s = jnp.dot(qbd, chunk_of(k_buf, c), preferred_element_type=F32) + bias
        vals_t = chunk_of(v_buf, c)
        return softmax_step(state, s, lambda p: lax.dot_general(p, vals_t, NT_DIMS, preferred_element_type=F32))

    hk = n_kv * HEAD_DIM
    state = (jnp.full((rows_a, 1), NEG_BIAS, F32), jnp.zeros((rows_a, 1), F32), jnp.zeros((rows_a, hk), F32))
    state = lax.fori_loop(0, n_chunks, attend_chunk, state)
    biasn = jnp.where(coln <= rown, bias_of(scn_ref[...], past + coln), NEG_BIAS)
    sn = lax.dot_general(qbd, kn_ref[0], NT_DIMS, preferred_element_type=F32)
    sn = sn + jnp.concatenate([biasn] * n_heads, axis=0)
    _, l_fin, acc = softmax_step(state, sn, lambda p: jnp.dot(p, vn_ref[0], preferred_element_type=F32))
    o = acc / l_fin
    for h in range(n_heads):
        g = h // group
        out_ref[:, h * HEAD_DIM:(h + 1) * HEAD_DIM] = o[h * t:(h + 1) * t, g * HEAD_DIM:(g + 1) * HEAD_DIM]
    o_ref[0] = out_ref[...]


def _sample_attention(page_table, qbd, qi, wi, ki_new, k_new, v_new, cache_idx, cache_k, cache_v,
                      *, t, kc, topk, n_kv, group):
    db, n_pages = page_table.shape
    past = n_pages * PAGE_SIZE
    hk = n_kv * HEAD_DIM
    hq = n_kv * group * HEAD_DIM
    blk = lambda b, pt: (b, 0, 0)
    any_spec = pl.BlockSpec(memory_space=pl.ANY)
    grid_spec = pltpu.PrefetchScalarGridSpec(
        num_scalar_prefetch=1,
        grid=(db,),
        in_specs=[pl.BlockSpec((1,) + qbd.shape[1:], blk), pl.BlockSpec((1,) + qi.shape[1:], blk),
                  pl.BlockSpec((1,) + wi.shape[1:], blk), pl.BlockSpec((1,) + ki_new.shape[1:], blk),
                  pl.BlockSpec((1,) + k_new.shape[1:], blk), pl.BlockSpec((1,) + v_new.shape[1:], blk),
                  any_spec, any_spec, any_spec],
        out_specs=pl.BlockSpec((1, t, hq), blk),
        scratch_shapes=[pltpu.VMEM((2, n_pages, IDX_DIM, PAGE_SIZE), F32), pltpu.VMEM((2, n_pages, hk, PAGE_SIZE), F32),
                        pltpu.VMEM((2, n_pages, hk, PAGE_SIZE), F32),
                        pltpu.VMEM((past // kc, t, kc), F32), pltpu.VMEM((t, LANES), F32),
                        pltpu.VMEM((t, hq), F32), pltpu.SemaphoreType.DMA((2, 3))],
    )
    return pl.pallas_call(
        functools.partial(_sattn_kernel, t=t, kc=kc, n_pages=n_pages, topk=topk, n_kv=n_kv, group=group,
                          idx_bits=max(past + LANES - 1, 1).bit_length()),
        grid_spec=grid_spec,
        out_shape=jax.ShapeDtypeStruct((db, t, hq), F32),
        compiler_params=_params("arbitrary"),
        name="sample_attention",
    )(page_table, qbd, qi, wi, ki_new, k_new, v_new, cache_idx, cache_k, cache_v)


def _merge_ffn_kernel(x_ref, conv_ref, attn_ref, wo_ref, g_ref, wg_ref, wu_ref, wd_ref, o_ref, *, c_conv):
    conv = conv_ref[...].astype(BF16)
    attn = attn_ref[...].astype(BF16)
    x = (x_ref[...] + jnp.dot(conv, wo_ref[:c_conv, :], preferred_element_type=F32)
         + jnp.dot(attn, wo_ref[c_conv:, :], preferred_element_type=F32))
    ms = jnp.mean(x * x, axis=-1, keepdims=True)
    h = ((x * lax.rsqrt(ms + 1e-6)) * g_ref[...]).astype(BF16)
    ffn = None
    for ci in range(wg_ref.shape[0]):
        gate = jnp.dot(h, wg_ref[ci], preferred_element_type=F32)
        up = jnp.dot(h, wu_ref[ci], preferred_element_type=F32)
        f = ((gate * jax.nn.sigmoid(gate)) * up).astype(BF16)
        part = jnp.dot(f, wd_ref[ci], preferred_element_type=F32)
        ffn = part if ffn is None else ffn + part
    o_ref[...] = x + ffn


def _merge_ffn(x2d, conv, attn, wo, g, wg, wu, wd, *, tm):
    t, d = x2d.shape
    c_conv = conv.shape[1]
    row = lambda i: (i, 0)
    return pl.pallas_call(
        functools.partial(_merge_ffn_kernel, c_conv=c_conv),
        grid=(t // tm,),
        in_specs=[pl.BlockSpec((tm, d), row), pl.BlockSpec((tm, c_conv), row), pl.BlockSpec((tm, attn.shape[1]), row),
                  _const_spec(wo.shape), _const_spec(g.shape), _const_spec(wg.shape), _const_spec(wu.shape),
                  _const_spec(wd.shape)],
        out_specs=pl.BlockSpec((tm, d), row),
        out_shape=jax.ShapeDtypeStruct((t, d), F32),
        compiler_params=_params("parallel"),
        name="merge_ffn",
    )(x2d, conv, attn, wo, g, wg, wu, wd)


def _ffn_chunks(d_ff):
    for n in (2, 3, 4):
        if d_ff % (n * LANES) == 0:
            return n
    return 1


def kernel(x_prompt, x_sample, cache_k, cache_v, cache_idx_k, state_conv, page_table, attn_norm_g, w_in,
           q_norm_g, k_norm_g, conv_dw_w, conv_dw_b, conv_ln_g, conv_ln_b, w_out, ffn_norm_g, w_gate, w_up,
           w_down):
    bsz, s, d = x_prompt.shape
    db, t, _ = x_sample.shape
    depth = w_in.shape[0]
    assert depth == 1, "single-layer step"
    n_pool = cache_k.shape[1]
    n_kv = cache_k.shape[3]
    c_conv = state_conv.shape[3]
    hk = n_kv * HEAD_DIM
    hq = w_out.shape[1] - c_conv
    group = hq // hk
    hi = N_IDX_HEADS * IDX_DIM
    d_main = 2 * c_conv + hq + 2 * hk + hi
    d_ff = w_gate.shape[2]
    past = page_table.shape[1] * PAGE_SIZE
    assert w_in.shape[2] == d_main + IDX_DIM + N_IDX_HEADS
    assert c_conv % LANES == 0 and hq % LANES == 0 and hk % LANES == 0
    topk_p = min(TOPK_MAX, s // 4)
    topk_s = min(TOPK_MAX, (past + t) // 4)

    tm = min(512, s)
    tq = min(256, s)
    assert s % tm == 0 and s % tq == 0 and (db * t) % min(tm, db * t) == 0 and tq >= topk_p

    w_tail = jnp.pad(w_in[0, :, d_main:], ((0, 0), (0, LANES - IDX_DIM - N_IDX_HEADS)))
    w_packed = jnp.concatenate([w_in[0, :, :d_main], w_tail], axis=1).astype(BF16)
    qg = jnp.tile(q_norm_g, (1, hq // HEAD_DIM))
    kg = jnp.tile(k_norm_g, (1, hk // HEAD_DIM))
    head_of = jnp.arange(hq) // HEAD_DIM
    bd = (head_of[:, None] == head_of[None, :]).astype(BF16)
    wo = w_out[0].astype(BF16)
    n_fc = _ffn_chunks(d_ff)
    col_chunks = lambda w: w.reshape(d, n_fc, d_ff // n_fc).transpose(1, 0, 2).astype(BF16)
    wg, wu = col_chunks(w_gate[0]), col_chunks(w_up[0])
    wd = w_down[0].reshape(n_fc, d_ff // n_fc, d).astype(BF16)
    dims = dict(c_conv=c_conv, hq=hq, hk=hk, hi=hi)

    xp2 = x_prompt.reshape(bsz * s, d)
    tabs_p = _rope_tables(jnp.arange(s, dtype=I32))
    u, q, k, kb, v, vb, qi, ki, kib, wi = _project(xp2, tabs_p, attn_norm_g, w_packed, qg, kg, bd, tm=tm, **dims)
    u3 = u.reshape(bsz, s, c_conv)
    conv_p = _conv_prompt(u3, conv_dw_w[0], conv_dw_b, conv_ln_g, conv_ln_b, ts=min(256, s), rt=64)
    n_heads = hq // HEAD_DIM
    heads_major = lambda a, n: a.reshape(bsz, s, n, HEAD_DIM).transpose(0, 2, 1, 3)
    vt = vb.reshape(bsz, s // tq, tq, hk).transpose(0, 1, 3, 2)
    attn_p = _prompt_attention(heads_major(q, n_heads), heads_major(qi, N_IDX_HEADS),
                               wi.reshape(bsz, s, N_IDX_HEADS).transpose(0, 2, 1),
                               kib.reshape(bsz, s, IDX_DIM), heads_major(kb, n_kv), vt,
                               tq=tq, topk=topk_p, n_kv=n_kv, group=group)
    y_p = _merge_ffn(xp2, conv_p.reshape(bsz * s, c_conv), attn_p.reshape(bsz * s, hq), wo, ffn_norm_g,
                     wg, wu, wd, tm=tm)

    ts_tok = db * t
    tms = min(tm, ts_tok)
    xs2 = x_sample.reshape(ts_tok, d)
    pos_s = past + (jnp.arange(tms, dtype=I32) % t)
    tabs_s = _rope_tables(pos_s)
    us, qs, ks, ksb, vs, vsb, qis, kis, kisb, wis = _project(xs2, tabs_s, attn_norm_g, w_packed, qg, kg, bd,
                                                             tm=tms, **dims)
    full_s = jnp.concatenate([state_conv[0], us.reshape(db, t, c_conv)], axis=1)
    conv_s = _conv_sample(full_s, conv_dw_w[0], conv_dw_b, conv_ln_g, conv_ln_b, bb=min(8, db), t=t)

    q_ht = qs.reshape(db, t, n_heads, HEAD_DIM).transpose(0, 2, 1, 3)
    kv_of = jnp.arange(n_heads) // group
    onehot = (kv_of[:, None] == jnp.arange(n_kv)[None, :]).astype(BF16)
    qbd = (q_ht[:, :, :, None, :] * onehot[None, :, None, :, None]).reshape(db, n_heads * t, hk)
    qi_ht = qis.reshape(db, t, N_IDX_HEADS, IDX_DIM).transpose(0, 2, 1, 3).reshape(db, N_IDX_HEADS * t, IDX_DIM)
    wi_ht = wis.reshape(db, t, N_IDX_HEADS).transpose(0, 2, 1).reshape(db, N_IDX_HEADS * t, 1)
    pad_rows = lambda a: jnp.pad(a.reshape(db, t, a.shape[-1]), ((0, 0), (0, LANES - t), (0, 0)))
    page_t = lambda c: jnp.moveaxis(c[0], 1, -1).reshape(n_pool, -1, PAGE_SIZE)
    attn_s = _sample_attention(page_table, qbd, qi_ht, wi_ht, pad_rows(kisb), pad_rows(ksb), pad_rows(vsb),
                               page_t(cache_idx_k), page_t(cache_k), page_t(cache_v),
                               t=t, kc=min(512, past), topk=topk_s, n_kv=n_kv, group=group)
    y_s = _merge_ffn(xs2, conv_s.reshape(ts_tok, c_conv), attn_s.reshape(ts_tok, hq), wo, ffn_norm_g,
                     wg, wu, wd, tm=tms)

    return (y_p.reshape(bsz, s, d), y_s.reshape(db, t, d),
            k.reshape(1, bsz, s, n_kv, HEAD_DIM), v.reshape(1, bsz, s, n_kv, HEAD_DIM),
            ki.reshape(1, bsz, s, IDX_DIM), u3[:, s - (CONV_WIDTH - 1):][None],
            ks.reshape(1, db, t, n_kv, HEAD_DIM), vs.reshape(1, db, t, n_kv, HEAD_DIM),
            kis.reshape(1, db, t, IDX_DIM), full_s[:, t:][None])
```

```python
import functools

import jax
import jax.numpy as jnp
from jax import lax
from jax.experimental import pallas as pl
from jax.experimental.pallas import tpu as pltpu

F32 = jnp.float32
BF16 = jnp.bfloat16
I32 = jnp.int32

HEAD_DIM = 64
IDX_DIM = 64
N_IDX_HEADS = 16
TOPK_MAX = 256
CONV_WIDTH = 31
ROPE_THETA = 500000.0
ROPE_FRACTION = 4
ROPE_HALF = HEAD_DIM // ROPE_FRACTION // 2
PAGE_SIZE = 128
LANES = 128
CONV_HALO = 32
INT_MIN = -(2 ** 31)
NEG_BIAS = -1e30
MAX_VALUE_PASSES = 48
VMEM_LIMIT = 56 * 1024 * 1024
NT_DIMS = (((1,), (1,)), ((), ()))


def _params(*sem, flags=None):
    return pltpu.CompilerParams(dimension_semantics=sem, vmem_limit_bytes=VMEM_LIMIT, flags=flags)


def _const_spec(shape):
    zeros = (0,) * len(shape)
    return pl.BlockSpec(shape, lambda *_: zeros)


def _rope_tables(pos):
    ang = pos.astype(F32)[:, None] * (ROPE_THETA ** (-jnp.arange(ROPE_HALF, dtype=F32) / ROPE_HALF))[None, :]
    cos, sin = jnp.cos(ang), jnp.sin(ang)
    t = pos.shape[0]
    rest = HEAD_DIM - 2 * ROPE_HALF
    c = jnp.concatenate([cos, cos, jnp.ones((t, rest), F32)], axis=1)
    a = jnp.concatenate([-sin, jnp.zeros((t, HEAD_DIM - ROPE_HALF), F32)], axis=1)
    b = jnp.concatenate([jnp.zeros((t, ROPE_HALF), F32), sin, jnp.zeros((t, rest), F32)], axis=1)
    rep = LANES // HEAD_DIM
    return tuple(jnp.tile(m, (1, rep)) for m in (c, a, b))


def _proj_kernel(x_ref, g_ref, w_ref, qg_ref, kg_ref, bd_ref, rc_ref, ra_ref, rb_ref,
                 u_ref, q_ref, k_ref, kb_ref, v_ref, vt_ref, qi_ref, ki_ref, kib_ref, wi_ref,
                 *, c_conv, hq, hk, hi):
    x = x_ref[...]
    ms = jnp.mean(x * x, axis=-1, keepdims=True)
    h = ((x * lax.rsqrt(ms + 1e-6)) * g_ref[...]).astype(BF16)
    rc, ra, rb = rc_ref[...], ra_ref[...], rb_ref[...]

    def seg(off, width):
        return jnp.dot(h, w_ref[:, off:off + width], preferred_element_type=F32)

    def rope(z):
        parts = []
        for j in range(z.shape[1] // LANES):
            zj = z[:, j * LANES:(j + 1) * LANES]
            parts.append(zj * rc + pltpu.roll(zj, LANES - ROPE_HALF, 1) * ra + pltpu.roll(zj, ROPE_HALF, 1) * rb)
        return parts[0] if len(parts) == 1 else jnp.concatenate(parts, axis=1)

    def head_norm(z, gain_ref):
        w = z.shape[1]
        s = z * z
        s_hi = s.astype(BF16)
        s_lo = (s - s_hi.astype(F32)).astype(BF16)
        bd = bd_ref[:w, :w]
        ssq = jnp.dot(s_hi, bd, preferred_element_type=F32) + jnp.dot(s_lo, bd, preferred_element_type=F32)
        return (z * lax.rsqrt(ssq * (1.0 / HEAD_DIM) + 1e-6)) * gain_ref[:, :w]

    off = 0
    u_val = seg(off, c_conv)
    off += c_conv
    u_gate = seg(off, c_conv)
    off += c_conv
    u_ref[...] = u_val * jax.nn.sigmoid(u_gate)

    def store_heads(ref, z):
        for hd in range(ref.shape[0]):
            ref[hd] = z[:, hd * HEAD_DIM:(hd + 1) * HEAD_DIM].astype(BF16)

    q = rope(head_norm(seg(off, hq), qg_ref))
    off += hq
    store_heads(q_ref, q * (HEAD_DIM ** -0.5))

    k = rope(head_norm(seg(off, hk), kg_ref))
    off += hk
    k_ref[...] = k
    store_heads(kb_ref, k)

    v = seg(off, hk)
    off += hk
    v_ref[...] = v
    vc = vt_ref.shape[2]
    for j in range(vt_ref.shape[0]):
        vt_ref[j] = v[j * vc:(j + 1) * vc, :].T.astype(BF16)

    store_heads(qi_ref, rope(seg(off, hi)))
    off += hi

    tail = seg(off, LANES)
    ki = rope(tail)[:, :IDX_DIM]
    ki_ref[...] = ki
    kib_ref[...] = ki.astype(BF16)
    wi_ref[...] = tail[:, IDX_DIM:IDX_DIM + N_IDX_HEADS] * ((N_IDX_HEADS * IDX_DIM) ** -0.5)


def _project(x2d, tabs, g, w_packed, qg, kg, bd, *, tm, vc, c_conv, hq, hk, hi):
    t, d = x2d.shape
    ntab = tabs[0].shape[0] // tm
    row = lambda i: (i, 0)
    mid = lambda i: (0, i, 0)
    tab_spec = pl.BlockSpec((tm, LANES), lambda i: (i % ntab, 0))

    def rows(width, dt):
        return pl.BlockSpec((tm, width), row), jax.ShapeDtypeStruct((t, width), dt)

    def heads(n):
        return pl.BlockSpec((n, tm, HEAD_DIM), mid), jax.ShapeDtypeStruct((n, t, HEAD_DIM), BF16)

    outs = [rows(c_conv, F32), heads(hq // HEAD_DIM), rows(hk, F32), heads(hk // HEAD_DIM), rows(hk, F32),
            (pl.BlockSpec((tm // vc, hk, vc), lambda i: (i, 0, 0)), jax.ShapeDtypeStruct((t // vc, hk, vc), BF16)),
            heads(hi // IDX_DIM), rows(IDX_DIM, F32), rows(IDX_DIM, BF16), rows(N_IDX_HEADS, F32)]
    return pl.pallas_call(
        functools.partial(_proj_kernel, c_conv=c_conv, hq=hq, hk=hk, hi=hi),
        grid=(t // tm,),
        in_specs=[pl.BlockSpec((tm, d), row), _const_spec(g.shape), _const_spec(w_packed.shape),
                  _const_spec(qg.shape), _const_spec(kg.shape), _const_spec(bd.shape),
                  tab_spec, tab_spec, tab_spec],
        out_specs=[spec for spec, _ in outs],
        out_shape=[shape for _, shape in outs],
        compiler_params=_params("parallel"),
        name="project",
    )(x2d, g, w_packed, qg, kg, bd, *tabs)


def _ln_swish(y, g, b):
    mu = jnp.mean(y, axis=-1, keepdims=True)
    yc = y - mu
    var = jnp.mean(yc * yc, axis=-1, keepdims=True)
    yn = (yc * lax.rsqrt(var + 1e-5)) * g + b
    return yn * jax.nn.sigmoid(yn)


def _conv_prompt_kernel(u_ref, up_ref, w_ref, b_ref, g_ref, beta_ref, o_ref, full_ref, *, ts, rt):
    i = pl.program_id(1)
    full_ref[0:CONV_HALO, :] = jnp.where(i > 0, up_ref[0], 0.0)
    full_ref[CONV_HALO:, :] = u_ref[0]
    base = CONV_HALO - (CONV_WIDTH - 1)
    for r0 in range(0, ts, rt):
        acc = jnp.zeros((rt, u_ref.shape[2]), F32)
        for j in range(CONV_WIDTH):
            acc = acc + full_ref[pl.ds(r0 + base + j, rt), :] * w_ref[j:j + 1, :]
        y = acc + b_ref[...]
        o_ref[0, r0:r0 + rt, :] = _ln_swish(y, g_ref[...], beta_ref[...]).astype(BF16)


def _conv_prompt(u, w, b, g, beta, *, ts, rt):
    bsz, s, c = u.shape
    per = ts // CONV_HALO
    return pl.pallas_call(
        functools.partial(_conv_prompt_kernel, ts=ts, rt=rt),
        grid=(bsz, s // ts),
        in_specs=[pl.BlockSpec((1, ts, c), lambda bi, i: (bi, i, 0)),
                  pl.BlockSpec((1, CONV_HALO, c), lambda bi, i: (bi, jnp.maximum(i * per - 1, 0), 0)),
                  _const_spec(w.shape), _const_spec(b.shape), _const_spec(g.shape), _const_spec(beta.shape)],
        out_specs=pl.BlockSpec((1, ts, c), lambda bi, i: (bi, i, 0)),
        out_shape=jax.ShapeDtypeStruct((bsz, s, c), BF16),
        scratch_shapes=[pltpu.VMEM((ts + CONV_HALO, c), F32)],
        compiler_params=_params("parallel", "parallel"),
        name="conv_prompt",
    )(u, u, w, b, g, beta)


def _conv_sample_kernel(full_ref, w_ref, b_ref, g_ref, beta_ref, o_ref, *, bb, t):
    for bi in range(bb):
        acc = jnp.zeros((t, full_ref.shape[2]), F32)
        for j in range(CONV_WIDTH):
            acc = acc + full_ref[bi, pl.ds(j, t), :] * w_ref[j:j + 1, :]
        y = acc + b_ref[...]
        o_ref[bi] = _ln_swish(y, g_ref[...], beta_ref[...])


def _conv_sample(full, w, b, g, beta, *, bb, t):
    db, rows, c = full.shape
    return pl.pallas_call(
        functools.partial(_conv_sample_kernel, bb=bb, t=t),
        grid=(db // bb,),
        in_specs=[pl.BlockSpec((bb, rows, c), lambda i: (i, 0, 0)),
                  _const_spec(w.shape), _const_spec(b.shape), _const_spec(g.shape), _const_spec(beta.shape)],
        out_specs=pl.BlockSpec((bb, t, c), lambda i: (i, 0, 0)),
        out_shape=jax.ShapeDtypeStruct((db, t, c), F32),
        compiler_params=_params("parallel"),
        name="conv_sample",
    )(full, w, b, g, beta)


def _key_to_score(key):
    return lax.bitcast_convert_type(jnp.where(key < 0, key ^ 0x7FFFFFFF, key), F32)


def _any(flag):
    return jnp.max(jnp.where(flag, 1.0, 0.0)) > 0.0


def _topk_cut(count, smin, smax, n_adm, total, topk, idx_bits):
    kf = float(topk)
    few = n_adm < kf
    lo = jnp.where(few, -jnp.inf, smin)
    c_lo = n_adm

    def probe(lo, hi, c_lo):
        cand = lo * 0.5 + hi * 0.5
        return cand, (cand > lo) & (cand < hi) & (c_lo != kf)

    def cond(st):
        return (st[0] < MAX_VALUE_PASSES) & st[4]

    def body(st):
        p, lo, hi, c_lo, _ = st
        cand, act = probe(lo, hi, c_lo)
        n = count(lambda s, idx: s >= cand)
        up = act & (n >= kf)
        down = act & (n < kf)
        lo, c_lo, hi = jnp.where(up, cand, lo), jnp.where(up, n, c_lo), jnp.where(down, cand, hi)
        return p + 1, lo, hi, c_lo, _any(probe(lo, hi, c_lo)[1])

    st = (jnp.int32(0), lo, smax, c_lo, _any(probe(lo, smax, c_lo)[1]))
    _, lo, _, c_lo, _ = lax.while_loop(cond, body, st)

    n_gt = count(lambda s, idx: s > lo)
    n_ge = jnp.where(few, total, c_lo)
    ok = (n_gt <= kf) & (n_ge >= kf)

    def bit_search():
        zero = jnp.zeros_like(lo)
        tau = jnp.where(count(lambda s, idx: s >= zero) >= kf, 0, INT_MIN).astype(I32)

        def bit_body(b, tau):
            cand = tau | lax.shift_left(jnp.int32(1), jnp.int32(30) - b)
            cand_f = _key_to_score(cand)
            return jnp.where(count(lambda s, idx: s >= cand_f) >= kf, cand, tau)

        tau_f = _key_to_score(lax.fori_loop(0, 31, bit_body, tau))
        tau_f = jnp.where(ok, lo, tau_f)
        return tau_f, count(lambda s, idx: s > tau_f), count(lambda s, idx: s >= tau_f)

    tau, n_gt, n_ge = lax.cond(_any(~ok), bit_search, lambda: (lo, n_gt, n_ge))
    need = kf - n_gt

    def tie_search():
        def body(b, p):
            cand = p + lax.shift_left(jnp.int32(1), jnp.int32(idx_bits - 1) - b)
            n = count(lambda s, idx: (s == tau) & (idx <= cand))
            return jnp.where(n < need, cand, p)
        p = lax.fori_loop(0, idx_bits, body, jnp.full(lo.shape, -1, I32))
        return jnp.where(need > 0.0, p + 1, -1)

    jlim = lax.cond(_any(n_ge > kf), tie_search, lambda: jnp.full(lo.shape, 2 ** idx_bits, I32))
    return tau, jlim


def _selected(score, kidx, tau, jlim):
    return (score > tau) | ((score == tau) & (kidx <= jlim))


def _canonical(score):
    return jnp.where(score == 0.0, 0.0, score)


def _sublane_fold(x, op):
    return op(x.reshape(x.shape[0] // 8, 8, x.shape[1]), axis=0)


def _pattn_kernel(q_ref, qi_ref, wi_ref, ki_ref, k_ref, vt_ref, o_ref, sc_ref, acc_ref,
                  *, tq, topk, n_kv, group, idx_bits):
    i = pl.program_id(1)
    n_chunks = i + 1
    krow = lax.broadcasted_iota(I32, (tq, tq), 0)
    qcol = lax.broadcasted_iota(I32, (tq, tq), 1)

    def index_chunk(c, carry):
        mn, mx, na = carry
        start = pl.multiple_of(c * tq, tq)
        kic = ki_ref[pl.ds(start, tq), :]
        acc = jnp.zeros((tq, tq), F32)
        for h in range(N_IDX_HEADS):
            x = lax.dot_general(kic, qi_ref[h], NT_DIMS, preferred_element_type=F32)
            acc = acc + wi_ref[0, h:h + 1, :] * jnp.maximum(x, 0.0)
        acc = _canonical(acc)
        future = (c == i) & (krow > qcol)
        sc = jnp.where(future, -jnp.inf, acc)
        sc_ref[c] = sc
        finite = sc > -jnp.inf
        mn = jnp.minimum(mn, _sublane_fold(jnp.where(finite, sc, jnp.inf), jnp.min))
        mx = jnp.maximum(mx, _sublane_fold(sc, jnp.max))
        na = na + _sublane_fold(jnp.where(finite, 1.0, 0.0), jnp.sum)
        return mn, mx, na

    mn, mx, na = lax.fori_loop(0, n_chunks, index_chunk,
                               (jnp.full((8, tq), jnp.inf, F32), jnp.full((8, tq), -jnp.inf, F32),
                                jnp.zeros((8, tq), F32)))
    smin = jnp.min(mn, axis=0, keepdims=True)
    smax = jnp.max(mx, axis=0, keepdims=True)
    n_adm = jnp.sum(na, axis=0, keepdims=True)
    total = (n_chunks * tq).astype(F32)

    def count(pred):
        def body(c, cnt):
            hit = pred(sc_ref[c], c * tq + krow)
            return cnt + _sublane_fold(jnp.where(hit, 1.0, 0.0), jnp.sum)
        cnt = lax.fori_loop(0, n_chunks, body, jnp.zeros((8, tq), F32))
        return jnp.sum(cnt, axis=0, keepdims=True)

    tau, jlim = _topk_cut(count, smin, smax, n_adm, total, topk, idx_bits)

    acc_ref[...] = jnp.zeros(acc_ref.shape, F32)
    gw = group * tq

    def attend_chunk(c, carry):
        start = pl.multiple_of(c * tq, tq)
        kidx = c * tq + krow
        sel = _selected(sc_ref[c], kidx, tau, jlim) & (kidx <= i * tq + qcol)
        bias = jnp.where(sel, 0.0, NEG_BIAS)
        bias = jnp.concatenate([bias] * group, axis=1)
        new = []
        for g in range(n_kv):
            m_old, l_old = carry[g]
            kg = k_ref[g, pl.ds(start, tq), :]
            qg = q_ref[g * group:(g + 1) * group].reshape(gw, HEAD_DIM)
            s = lax.dot_general(kg, qg, NT_DIMS, preferred_element_type=F32) + bias
            m_new = jnp.maximum(m_old, jnp.max(_sublane_fold(s, jnp.max), axis=0, keepdims=True))
            p = jnp.exp(s - m_new)
            alpha = jnp.exp(m_old - m_new)
            l_new = alpha * l_old + jnp.sum(_sublane_fold(p, jnp.sum), axis=0, keepdims=True)
            pv = jnp.dot(vt_ref[c, g * HEAD_DIM:(g + 1) * HEAD_DIM, :], p.astype(BF16),
                         preferred_element_type=F32)
            acc_ref[g] = alpha * acc_ref[g] + pv
            new.append((m_new, l_new))
        return tuple(new)

    init = tuple((jnp.full((1, gw), NEG_BIAS, F32), jnp.zeros((1, gw), F32)) for _ in range(n_kv))
    fin = lax.fori_loop(0, n_chunks, attend_chunk, init)

    outs = []
    for g in range(n_kv):
        og = acc_ref[g] / fin[g][1]
        for r in range(group):
            outs.append(og[:, r * tq:(r + 1) * tq])
    o_ref[0] = jnp.concatenate(outs, axis=0).T.astype(BF16)


def _prompt_attention(q, qi, wi_t, kib, kb, vt, *, bsz, tq, topk, n_kv, group):
    n_heads, tokens, _ = q.shape
    s = tokens // bsz
    nb = s // tq
    hk = n_kv * HEAD_DIM
    hq = n_heads * HEAD_DIM
    return pl.pallas_call(
        functools.partial(_pattn_kernel, tq=tq, topk=topk, n_kv=n_kv, group=group,
                          idx_bits=max(s - 1, 1).bit_length()),
        grid=(bsz, nb),
        in_specs=[pl.BlockSpec((n_heads, tq, HEAD_DIM), lambda bi, i: (0, bi * nb + i, 0)),
                  pl.BlockSpec((N_IDX_HEADS, tq, IDX_DIM), lambda bi, i: (0, bi * nb + i, 0)),
                  pl.BlockSpec((1, N_IDX_HEADS, tq), lambda bi, i: (bi, 0, i)),
                  pl.BlockSpec((s, IDX_DIM), lambda bi, i: (bi, 0)),
                  pl.BlockSpec((n_kv, s, HEAD_DIM), lambda bi, i: (0, bi, 0)),
                  pl.BlockSpec((nb, hk, tq), lambda bi, i: (bi, 0, 0))],
        out_specs=pl.BlockSpec((1, tq, hq), lambda bi, i: (bi, i, 0)),
        out_shape=jax.ShapeDtypeStruct((bsz, s, hq), BF16),
        scratch_shapes=[pltpu.VMEM((s // tq, tq, tq), F32),
                        pltpu.VMEM((n_kv, HEAD_DIM, group * tq), F32)],
        compiler_params=_params("parallel", "arbitrary"),
        name="prompt_attention",
    )(q, qi, wi_t, kib, kb, vt)


def _lane_fold(m, op):
    out = m[:, :LANES]
    for j in range(1, m.shape[1] // LANES):
        out = op(out, m[:, j * LANES:(j + 1) * LANES])
    return out


def _sattn_kernel(pt_ref, qbd_ref, qi_ref, wi_ref, kin_ref, kn_ref, vn_ref, cidx_ref, ck_ref, cv_ref,
                  o_ref, idx_buf, k_buf, v_buf, sc_ref, scn_ref, out_ref, sem,
                  *, t, kc, n_pages, topk, n_kv, group, idx_bits):
    b = pl.program_id(0)
    past = n_pages * PAGE_SIZE
    n_chunks = past // kc
    ppc = kc // PAGE_SIZE
    n_heads = n_kv * group
    rows_a = n_heads * t
    slot = b % 2

    def page_copies(batch, sl, p):
        page = pt_ref[batch, p]
        return (pltpu.make_async_copy(cidx_ref.at[page], idx_buf.at[sl, p], sem.at[sl, 0]),
                pltpu.make_async_copy(ck_ref.at[page], k_buf.at[sl, p], sem.at[sl, 1]),
                pltpu.make_async_copy(cv_ref.at[page], v_buf.at[sl, p], sem.at[sl, 2]))

    def start_batch(batch, sl):
        def body(p, carry):
            for cp in page_copies(batch, sl, p):
                cp.start()
            return carry
        lax.fori_loop(0, n_pages, body, 0)

    @pl.when(b == 0)
    def _():
        start_batch(0, 0)

    @pl.when(b + 1 < pl.num_programs(0))
    def _():
        start_batch(b + 1, 1 - slot)

    def wait_page(p, carry):
        for cp in page_copies(b, slot, p):
            cp.wait()
        return carry

    lax.fori_loop(0, n_pages, wait_page, 0)

    def chunk_of(buf, c):
        return jnp.concatenate([buf[slot, c * ppc + j] for j in range(ppc)], axis=1).astype(BF16)

    qi = qi_ref[0]
    wi = wi_ref[0]

    def head_sum(y):
        out = y[0:t]
        for h in range(1, N_IDX_HEADS):
            out = out + y[h * t:(h + 1) * t]
        return out

    def index_chunk(c, carry):
        mn, mx, na = carry
        x = jnp.dot(qi, chunk_of(idx_buf, c), preferred_element_type=F32)
        sc = _canonical(head_sum(wi * jnp.maximum(x, 0.0)))
        sc_ref[c] = sc
        finite = sc > -jnp.inf
        return (jnp.minimum(mn, _lane_fold(jnp.where(finite, sc, jnp.inf), jnp.minimum)),
                jnp.maximum(mx, _lane_fold(sc, jnp.maximum)),
                na + _lane_fold(jnp.where(finite, 1.0, 0.0), jnp.add))

    mn, mx, na = lax.fori_loop(0, n_chunks, index_chunk,
                               (jnp.full((t, LANES), jnp.inf, F32), jnp.full((t, LANES), -jnp.inf, F32),
                                jnp.zeros((t, LANES), F32)))

    rown = lax.broadcasted_iota(I32, (t, LANES), 0)
    coln = lax.broadcasted_iota(I32, (t, LANES), 1)
    xn = lax.dot_general(qi, kin_ref[0], NT_DIMS, preferred_element_type=F32)
    scn = _canonical(head_sum(wi * jnp.maximum(xn, 0.0)))
    scn = jnp.where(coln <= rown, scn, -jnp.inf)
    scn_ref[...] = scn
    smin = jnp.min(jnp.minimum(mn, jnp.where(scn > -jnp.inf, scn, jnp.inf)), axis=1, keepdims=True)
    smax = jnp.max(jnp.maximum(mx, scn), axis=1, keepdims=True)
    n_adm = jnp.sum(na + jnp.where(scn > -jnp.inf, 1.0, 0.0), axis=1, keepdims=True)
    total = float(past + LANES)

    colc = lax.broadcasted_iota(I32, (t, kc), 1)

    def count(pred):
        def body(c, cnt):
            hit = pred(sc_ref[c], c * kc + colc)
            return cnt + _lane_fold(jnp.where(hit, 1.0, 0.0), jnp.add)
        cnt = lax.fori_loop(0, n_chunks, body, jnp.zeros((t, LANES), F32))
        cnt = cnt + jnp.where(pred(scn_ref[...], past + coln), 1.0, 0.0)
        return jnp.sum(cnt, axis=1, keepdims=True)

    tau, jlim = _topk_cut(count, smin, smax, n_adm, total, topk, idx_bits)

    def bias_of(sc, kpos):
        return jnp.where(_selected(sc, kpos, tau, jlim), 0.0, NEG_BIAS)

    qbd = qbd_ref[0]

    def softmax_step(state, s, weigh):
        m_old, l_old, acc = state
        m_new = jnp.maximum(m_old, jnp.max(s, axis=1, keepdims=True))
        p = jnp.exp(s - m_new)
        alpha = jnp.exp(m_old - m_new)
        l_new = alpha * l_old + jnp.sum(p, axis=1, keepdims=True)
        return m_new, l_new, alpha * acc + weigh(p.astype(BF16))

    def attend_chunk(c, state):
        bias = jnp.concatenate([bias_of(sc_ref[c], c * kc + colc)] * n_heads, axis=0)
        s = jnp.dot(qbd, chunk_of(k_buf, c), preferred_element_type=F32) + bias
        vals_t = chunk_of(v_buf, c)
        return softmax_step(state, s, lambda p: lax.dot_general(p, vals_t, NT_DIMS, preferred_element_type=F32))

    hk = n_kv * HEAD_DIM
    state = (jnp.full((rows_a, 1), NEG_BIAS, F32), jnp.zeros((rows_a, 1), F32), jnp.zeros((rows_a, hk), F32))
    state = lax.fori_loop(0, n_chunks, attend_chunk, state)
    biasn = jnp.where(coln <= rown, bias_of(scn_ref[...], past + coln), NEG_BIAS)
    sn = lax.dot_general(qbd, kn_ref[0], NT_DIMS, preferred_element_type=F32)
    sn = sn + jnp.concatenate([biasn] * n_heads, axis=0)
    _, l_fin, acc = softmax_step(state, sn, lambda p: jnp.dot(p, vn_ref[0], preferred_element_type=F32))
    o = acc / l_fin
    for h in range(n_heads):
        g = h // group
        out_ref[:, h * HEAD_DIM:(h + 1) * HEAD_DIM] = o[h * t:(h + 1) * t, g * HEAD_DIM:(g + 1) * HEAD_DIM]
    o_ref[0] = out_ref[...]


def _sample_attention(page_table, qbd, qi, wi, ki_new, k_new, v_new, cache_idx, cache_k, cache_v,
                      *, t, kc, topk, n_kv, group):
    db, n_pages = page_table.shape
    past = n_pages * PAGE_SIZE
    hk = n_kv * HEAD_DIM
    hq = n_kv * group * HEAD_DIM
    blk = lambda b, pt: (b, 0, 0)
    any_spec = pl.BlockSpec(memory_space=pl.ANY)
    grid_spec = pltpu.PrefetchScalarGridSpec(
        num_scalar_prefetch=1,
        grid=(db,),
        in_specs=[pl.BlockSpec((1,) + qbd.shape[1:], blk), pl.BlockSpec((1,) + qi.shape[1:], blk),
                  pl.BlockSpec((1,) + wi.shape[1:], blk), pl.BlockSpec((1,) + ki_new.shape[1:], blk),
                  pl.BlockSpec((1,) + k_new.shape[1:], blk), pl.BlockSpec((1,) + v_new.shape[1:], blk),
                  any_spec, any_spec, any_spec],
        out_specs=pl.BlockSpec((1, t, hq), blk),
        scratch_shapes=[pltpu.VMEM((2, n_pages, IDX_DIM, PAGE_SIZE), F32), pltpu.VMEM((2, n_pages, hk, PAGE_SIZE), F32),
                        pltpu.VMEM((2, n_pages, hk, PAGE_SIZE), F32),
                        pltpu.VMEM((past // kc, t, kc), F32), pltpu.VMEM((t, LANES), F32),
                        pltpu.VMEM((t, hq), F32), pltpu.SemaphoreType.DMA((2, 3))],
    )
    return pl.pallas_call(
        functools.partial(_sattn_kernel, t=t, kc=kc, n_pages=n_pages, topk=topk, n_kv=n_kv, group=group,
                          idx_bits=max(past + LANES - 1, 1).bit_length()),
        grid_spec=grid_spec,
        out_shape=jax.ShapeDtypeStruct((db, t, hq), F32),
        compiler_params=_params("arbitrary"),
        name="sample_attention",
    )(page_table, qbd, qi, wi, ki_new, k_new, v_new, cache_idx, cache_k, cache_v)


def _merge_ffn_kernel(x_ref, conv_ref, attn_ref, wo_ref, g_ref, wg_ref, wu_ref, wd_ref, o_ref, *, c_conv):
    conv = conv_ref[...].astype(BF16)
    attn = attn_ref[...].astype(BF16)
    x = (x_ref[...] + jnp.dot(conv, wo_ref[:c_conv, :], preferred_element_type=F32)
         + jnp.dot(attn, wo_ref[c_conv:, :], preferred_element_type=F32))
    ms = jnp.mean(x * x, axis=-1, keepdims=True)
    h = ((x * lax.rsqrt(ms + 1e-6)) * g_ref[...]).astype(BF16)
    ffn = None
    for ci in range(wg_ref.shape[0]):
        gate = jnp.dot(h, wg_ref[ci], preferred_element_type=F32)
        up = jnp.dot(h, wu_ref[ci], preferred_element_type=F32)
        f = ((gate * jax.nn.sigmoid(gate)) * up).astype(BF16)
        part = jnp.dot(f, wd_ref[ci], preferred_element_type=F32)
        ffn = part if ffn is None else ffn + part
    o_ref[...] = x + ffn


def _merge_ffn(x2d, conv, attn, wo, g, wg, wu, wd, *, tm):
    t, d = x2d.shape
    c_conv = conv.shape[1]
    row = lambda i: (i, 0)
    return pl.pallas_call(
        functools.partial(_merge_ffn_kernel, c_conv=c_conv),
        grid=(t // tm,),
        in_specs=[pl.BlockSpec((tm, d), row), pl.BlockSpec((tm, c_conv), row), pl.BlockSpec((tm, attn.shape[1]), row),
                  _const_spec(wo.shape), _const_spec(g.shape), _const_spec(wg.shape), _const_spec(wu.shape),
                  _const_spec(wd.shape)],
        out_specs=pl.BlockSpec((tm, d), row),
        out_shape=jax.ShapeDtypeStruct((t, d), F32),
        compiler_params=_params("parallel"),
        name="merge_ffn",
    )(x2d, conv, attn, wo, g, wg, wu, wd)


def _ffn_chunks(d_ff):
    for n in (2, 3, 4):
        if d_ff % (n * LANES) == 0:
            return n
    return 1


def kernel(x_prompt, x_sample, cache_k, cache_v, cache_idx_k, state_conv, page_table, attn_norm_g, w_in,
           q_norm_g, k_norm_g, conv_dw_w, conv_dw_b, conv_ln_g, conv_ln_b, w_out, ffn_norm_g, w_gate, w_up,
           w_down):
    bsz, s, d = x_prompt.shape
    db, t, _ = x_sample.shape
    depth = w_in.shape[0]
    assert depth == 1, "single-layer step"
    n_pool = cache_k.shape[1]
    n_kv = cache_k.shape[3]
    c_conv = state_conv.shape[3]
    hk = n_kv * HEAD_DIM
    hq = w_out.shape[1] - c_conv
    group = hq // hk
    hi = N_IDX_HEADS * IDX_DIM
    d_main = 2 * c_conv + hq + 2 * hk + hi
    d_ff = w_gate.shape[2]
    past = page_table.shape[1] * PAGE_SIZE
    assert w_in.shape[2] == d_main + IDX_DIM + N_IDX_HEADS
    assert c_conv % LANES == 0 and hq % LANES == 0 and hk % LANES == 0
    topk_p = min(TOPK_MAX, s // 4)
    topk_s = min(TOPK_MAX, (past + t) // 4)

    tm = min(512, s)
    tq = min(256, s)
    assert s % tm == 0 and s % tq == 0 and (db * t) % min(tm, db * t) == 0 and tq >= topk_p

    w_tail = jnp.pad(w_in[0, :, d_main:], ((0, 0), (0, LANES - IDX_DIM - N_IDX_HEADS)))
    w_packed = jnp.concatenate([w_in[0, :, :d_main], w_tail], axis=1).astype(BF16)
    qg = jnp.tile(q_norm_g, (1, hq // HEAD_DIM))
    kg = jnp.tile(k_norm_g, (1, hk // HEAD_DIM))
    head_of = jnp.arange(hq) // HEAD_DIM
    bd = (head_of[:, None] == head_of[None, :]).astype(BF16)
    wo = w_out[0].astype(BF16)
    n_fc = _ffn_chunks(d_ff)
    col_chunks = lambda w: w.reshape(d, n_fc, d_ff // n_fc).transpose(1, 0, 2).astype(BF16)
    wg, wu = col_chunks(w_gate[0]), col_chunks(w_up[0])
    wd = w_down[0].reshape(n_fc, d_ff // n_fc, d).astype(BF16)
    dims = dict(c_conv=c_conv, hq=hq, hk=hk, hi=hi)

    xp2 = x_prompt.reshape(bsz * s, d)
    tabs_p = _rope_tables(jnp.arange(s, dtype=I32))
    u, q, k, kb, v, vt, qi, ki, kib, wi = _project(xp2, tabs_p, attn_norm_g, w_packed, qg, kg, bd, tm=tm, vc=tq,
                                                   **dims)
    u3 = u.reshape(bsz, s, c_conv)
    conv_p = _conv_prompt(u3, conv_dw_w[0], conv_dw_b, conv_ln_g, conv_ln_b, ts=min(256, s), rt=64)
    n_heads = hq // HEAD_DIM
    attn_p = _prompt_attention(q, qi, wi.reshape(bsz, s, N_IDX_HEADS).transpose(0, 2, 1), kib, kb, vt,
                               bsz=bsz, tq=tq, topk=topk_p, n_kv=n_kv, group=group)
    y_p = _merge_ffn(xp2, conv_p.reshape(bsz * s, c_conv), attn_p.reshape(bsz * s, hq), wo, ffn_norm_g,
                     wg, wu, wd, tm=tm)

    ts_tok = db * t
    tms = min(tm, ts_tok)
    xs2 = x_sample.reshape(ts_tok, d)
    pos_s = past + (jnp.arange(tms, dtype=I32) % t)
    tabs_s = _rope_tables(pos_s)
    us, qs, ks, _, vs, _, qis, kis, kisb, wis = _project(xs2, tabs_s, attn_norm_g, w_packed, qg, kg, bd,
                                                         tm=tms, vc=min(tq, tms), **dims)
    full_s = jnp.concatenate([state_conv[0], us.reshape(db, t, c_conv)], axis=1)
    conv_s = _conv_sample(full_s, conv_dw_w[0], conv_dw_b, conv_ln_g, conv_ln_b, bb=min(8, db), t=t)

    q_ht = qs.reshape(n_heads, db, t, HEAD_DIM).transpose(1, 0, 2, 3)
    kv_of = jnp.arange(n_heads) // group
    onehot = (kv_of[:, None] == jnp.arange(n_kv)[None, :]).astype(BF16)
    qbd = (q_ht[:, :, :, None, :] * onehot[None, :, None, :, None]).reshape(db, n_heads * t, hk)
    qi_ht = qis.reshape(N_IDX_HEADS, db, t, IDX_DIM).transpose(1, 0, 2, 3).reshape(db, N_IDX_HEADS * t, IDX_DIM)
    wi_ht = wis.reshape(db, t, N_IDX_HEADS).transpose(0, 2, 1).reshape(db, N_IDX_HEADS * t, 1)
    pad_rows = lambda a: jnp.pad(a.reshape(db, t, a.shape[-1]), ((0, 0), (0, LANES - t), (0, 0)))
    ksb, vsb = ks.astype(BF16), vs.astype(BF16)
    page_t = lambda c: jnp.moveaxis(c[0], 1, -1).reshape(n_pool, -1, PAGE_SIZE)
    attn_s = _sample_attention(page_table, qbd, qi_ht, wi_ht, pad_rows(kisb), pad_rows(ksb), pad_rows(vsb),
                               page_t(cache_idx_k), page_t(cache_k), page_t(cache_v),
                               t=t, kc=min(512, past), topk=topk_s, n_kv=n_kv, group=group)
    y_s = _merge_ffn(xs2, conv_s.reshape(ts_tok, c_conv), attn_s.reshape(ts_tok, hq), wo, ffn_norm_g,
                     wg, wu, wd, tm=tms)

    return (y_p.reshape(bsz, s, d), y_s.reshape(db, t, d),
            k.reshape(1, bsz, s, n_kv, HEAD_DIM), v.reshape(1, bsz, s, n_kv, HEAD_DIM),
            ki.reshape(1, bsz, s, IDX_DIM), u3[:, s - (CONV_WIDTH - 1):][None],
            ks.reshape(1, db, t, n_kv, HEAD_DIM), vs.reshape(1, db, t, n_kv, HEAD_DIM),
            kis.reshape(1, db, t, IDX_DIM), full_s[:, t:][None])
```

```python
import functools

import jax
import jax.numpy as jnp
from jax import lax
from jax.experimental import pallas as pl
from jax.experimental.pallas import tpu as pltpu

F32 = jnp.float32
BF16 = jnp.bfloat16
I32 = jnp.int32

HEAD_DIM = 64
IDX_DIM = 64
N_IDX_HEADS = 16
TOPK_MAX = 256
CONV_WIDTH = 31
ROPE_THETA = 500000.0
ROPE_FRACTION = 4
ROPE_HALF = HEAD_DIM // ROPE_FRACTION // 2
PAGE_SIZE = 128
LANES = 128
CONV_HALO = 32
INT_MIN = -(2 ** 31)
NEG_BIAS = -1e30
MAX_VALUE_PASSES = 48
VMEM_LIMIT = 56 * 1024 * 1024
NT_DIMS = (((1,), (1,)), ((), ()))


def _params(*sem, flags=None):
    return pltpu.CompilerParams(dimension_semantics=sem, vmem_limit_bytes=VMEM_LIMIT, flags=flags)


def _const_spec(shape):
    zeros = (0,) * len(shape)
    return pl.BlockSpec(shape, lambda *_: zeros)


def _rope_tables(pos):
    ang = pos.astype(F32)[:, None] * (ROPE_THETA ** (-jnp.arange(ROPE_HALF, dtype=F32) / ROPE_HALF))[None, :]
    cos, sin = jnp.cos(ang), jnp.sin(ang)
    t = pos.shape[0]
    rest = HEAD_DIM - 2 * ROPE_HALF
    c = jnp.concatenate([cos, cos, jnp.ones((t, rest), F32)], axis=1)
    a = jnp.concatenate([-sin, jnp.zeros((t, HEAD_DIM - ROPE_HALF), F32)], axis=1)
    b = jnp.concatenate([jnp.zeros((t, ROPE_HALF), F32), sin, jnp.zeros((t, rest), F32)], axis=1)
    rep = LANES // HEAD_DIM
    return tuple(jnp.tile(m, (1, rep)) for m in (c, a, b))


def _proj_kernel(x_ref, g_ref, w_ref, qg_ref, kg_ref, bd_ref, rc_ref, ra_ref, rb_ref,
                 u_ref, q_ref, k_ref, kb_ref, v_ref, vt_ref, qi_ref, ki_ref, kib_ref, wi_ref,
                 *, c_conv, hq, hk, hi):
    x = x_ref[...]
    ms = jnp.mean(x * x, axis=-1, keepdims=True)
    h = ((x * lax.rsqrt(ms + 1e-6)) * g_ref[...]).astype(BF16)
    rc, ra, rb = rc_ref[...], ra_ref[...], rb_ref[...]

    def seg(off, width):
        return jnp.dot(h, w_ref[:, off:off + width], preferred_element_type=F32)

    def rope(z):
        parts = []
        for j in range(z.shape[1] // LANES):
            zj = z[:, j * LANES:(j + 1) * LANES]
            parts.append(zj * rc + pltpu.roll(zj, LANES - ROPE_HALF, 1) * ra + pltpu.roll(zj, ROPE_HALF, 1) * rb)
        return parts[0] if len(parts) == 1 else jnp.concatenate(parts, axis=1)

    def head_norm(z, gain_ref):
        w = z.shape[1]
        s = z * z
        s_hi = s.astype(BF16)
        s_lo = (s - s_hi.astype(F32)).astype(BF16)
        bd = bd_ref[:w, :w]
        ssq = jnp.dot(s_hi, bd, preferred_element_type=F32) + jnp.dot(s_lo, bd, preferred_element_type=F32)
        return (z * lax.rsqrt(ssq * (1.0 / HEAD_DIM) + 1e-6)) * gain_ref[:, :w]

    off = 0
    u_val = seg(off, c_conv)
    off += c_conv
    u_gate = seg(off, c_conv)
    off += c_conv
    u_ref[...] = u_val * jax.nn.sigmoid(u_gate)

    def store_heads(ref, z):
        for hd in range(ref.shape[0]):
            ref[hd] = z[:, hd * HEAD_DIM:(hd + 1) * HEAD_DIM].astype(BF16)

    q = rope(head_norm(seg(off, hq), qg_ref))
    off += hq
    store_heads(q_ref, q * (HEAD_DIM ** -0.5))

    k = rope(head_norm(seg(off, hk), kg_ref))
    off += hk
    k_ref[...] = k
    store_heads(kb_ref, k)

    v = seg(off, hk)
    off += hk
    v_ref[...] = v
    vc = vt_ref.shape[2]
    for j in range(vt_ref.shape[0]):
        vt_ref[j] = v[j * vc:(j + 1) * vc, :].T.astype(BF16)

    store_heads(qi_ref, rope(seg(off, hi)))
    off += hi

    tail = seg(off, LANES)
    ki = rope(tail)[:, :IDX_DIM]
    ki_ref[...] = ki
    kib_ref[...] = ki.astype(BF16)
    wi_ref[...] = tail[:, IDX_DIM:IDX_DIM + N_IDX_HEADS] * ((N_IDX_HEADS * IDX_DIM) ** -0.5)


def _project(x2d, tabs, g, w_packed, qg, kg, bd, *, tm, vc, c_conv, hq, hk, hi):
    t, d = x2d.shape
    ntab = tabs[0].shape[0] // tm
    row = lambda i: (i, 0)
    mid = lambda i: (0, i, 0)
    tab_spec = pl.BlockSpec((tm, LANES), lambda i: (i % ntab, 0))

    def rows(width, dt):
        return pl.BlockSpec((tm, width), row), jax.ShapeDtypeStruct((t, width), dt)

    def heads(n):
        return pl.BlockSpec((n, tm, HEAD_DIM), mid), jax.ShapeDtypeStruct((n, t, HEAD_DIM), BF16)

    outs = [rows(c_conv, F32), heads(hq // HEAD_DIM), rows(hk, F32), heads(hk // HEAD_DIM), rows(hk, F32),
            (pl.BlockSpec((tm // vc, hk, vc), lambda i: (i, 0, 0)), jax.ShapeDtypeStruct((t // vc, hk, vc), BF16)),
            heads(hi // IDX_DIM), rows(IDX_DIM, F32), rows(IDX_DIM, BF16), rows(N_IDX_HEADS, F32)]
    return pl.pallas_call(
        functools.partial(_proj_kernel, c_conv=c_conv, hq=hq, hk=hk, hi=hi),
        grid=(t // tm,),
        in_specs=[pl.BlockSpec((tm, d), row), _const_spec(g.shape), _const_spec(w_packed.shape),
                  _const_spec(qg.shape), _const_spec(kg.shape), _const_spec(bd.shape),
                  tab_spec, tab_spec, tab_spec],
        out_specs=[spec for spec, _ in outs],
        out_shape=[shape for _, shape in outs],
        compiler_params=_params("parallel"),
        name="project",
    )(x2d, g, w_packed, qg, kg, bd, *tabs)


def _ln_swish(y, g, b):
    mu = jnp.mean(y, axis=-1, keepdims=True)
    yc = y - mu
    var = jnp.mean(yc * yc, axis=-1, keepdims=True)
    yn = (yc * lax.rsqrt(var + 1e-5)) * g + b
    return yn * jax.nn.sigmoid(yn)


def _conv_prompt_kernel(u_ref, up_ref, w_ref, b_ref, g_ref, beta_ref, o_ref, full_ref, *, ts, rt):
    i = pl.program_id(1)
    full_ref[0:CONV_HALO, :] = jnp.where(i > 0, up_ref[0], 0.0)
    full_ref[CONV_HALO:, :] = u_ref[0]
    base = CONV_HALO - (CONV_WIDTH - 1)
    for r0 in range(0, ts, rt):
        acc = jnp.zeros((rt, u_ref.shape[2]), F32)
        for j in range(CONV_WIDTH):
            acc = acc + full_ref[pl.ds(r0 + base + j, rt), :] * w_ref[j:j + 1, :]
        y = acc + b_ref[...]
        o_ref[0, r0:r0 + rt, :] = _ln_swish(y, g_ref[...], beta_ref[...]).astype(BF16)


def _conv_prompt(u, w, b, g, beta, *, ts, rt):
    bsz, s, c = u.shape
    per = ts // CONV_HALO
    return pl.pallas_call(
        functools.partial(_conv_prompt_kernel, ts=ts, rt=rt),
        grid=(bsz, s // ts),
        in_specs=[pl.BlockSpec((1, ts, c), lambda bi, i: (bi, i, 0)),
                  pl.BlockSpec((1, CONV_HALO, c), lambda bi, i: (bi, jnp.maximum(i * per - 1, 0), 0)),
                  _const_spec(w.shape), _const_spec(b.shape), _const_spec(g.shape), _const_spec(beta.shape)],
        out_specs=pl.BlockSpec((1, ts, c), lambda bi, i: (bi, i, 0)),
        out_shape=jax.ShapeDtypeStruct((bsz, s, c), BF16),
        scratch_shapes=[pltpu.VMEM((ts + CONV_HALO, c), F32)],
        compiler_params=_params("parallel", "parallel"),
        name="conv_prompt",
    )(u, u, w, b, g, beta)


def _conv_sample_kernel(full_ref, w_ref, b_ref, g_ref, beta_ref, o_ref, *, bb, t):
    for bi in range(bb):
        acc = jnp.zeros((t, full_ref.shape[2]), F32)
        for j in range(CONV_WIDTH):
            acc = acc + full_ref[bi, pl.ds(j, t), :] * w_ref[j:j + 1, :]
        y = acc + b_ref[...]
        o_ref[bi] = _ln_swish(y, g_ref[...], beta_ref[...])


def _conv_sample(full, w, b, g, beta, *, bb, t):
    db, rows, c = full.shape
    return pl.pallas_call(
        functools.partial(_conv_sample_kernel, bb=bb, t=t),
        grid=(db // bb,),
        in_specs=[pl.BlockSpec((bb, rows, c), lambda i: (i, 0, 0)),
                  _const_spec(w.shape), _const_spec(b.shape), _const_spec(g.shape), _const_spec(beta.shape)],
        out_specs=pl.BlockSpec((bb, t, c), lambda i: (i, 0, 0)),
        out_shape=jax.ShapeDtypeStruct((db, t, c), F32),
        compiler_params=_params("parallel"),
        name="conv_sample",
    )(full, w, b, g, beta)


def _key_to_score(key):
    return lax.bitcast_convert_type(jnp.where(key < 0, key ^ 0x7FFFFFFF, key), F32)


def _any(flag):
    return jnp.max(jnp.where(flag, 1.0, 0.0)) > 0.0


def _topk_cut(count, smin, smax, n_adm, total, topk, idx_bits):
    kf = float(topk)
    few = n_adm < kf
    lo = jnp.where(few, -jnp.inf, smin)
    c_lo = n_adm

    def probe(lo, hi, c_lo):
        cand = lo * 0.5 + hi * 0.5
        return cand, (cand > lo) & (cand < hi) & (c_lo != kf)

    def cond(st):
        return (st[0] < MAX_VALUE_PASSES) & st[4]

    def body(st):
        p, lo, hi, c_lo, _ = st
        cand, act = probe(lo, hi, c_lo)
        n = count(lambda s, idx: s >= cand)
        up = act & (n >= kf)
        down = act & (n < kf)
        lo, c_lo, hi = jnp.where(up, cand, lo), jnp.where(up, n, c_lo), jnp.where(down, cand, hi)
        return p + 1, lo, hi, c_lo, _any(probe(lo, hi, c_lo)[1])

    st = (jnp.int32(0), lo, smax, c_lo, _any(probe(lo, smax, c_lo)[1]))
    _, lo, _, c_lo, _ = lax.while_loop(cond, body, st)

    n_gt = count(lambda s, idx: s > lo)
    n_ge = jnp.where(few, total, c_lo)
    ok = (n_gt <= kf) & (n_ge >= kf)

    def bit_search():
        zero = jnp.zeros_like(lo)
        tau = jnp.where(count(lambda s, idx: s >= zero) >= kf, 0, INT_MIN).astype(I32)

        def bit_body(b, tau):
            cand = tau | lax.shift_left(jnp.int32(1), jnp.int32(30) - b)
            cand_f = _key_to_score(cand)
            return jnp.where(count(lambda s, idx: s >= cand_f) >= kf, cand, tau)

        tau_f = _key_to_score(lax.fori_loop(0, 31, bit_body, tau))
        tau_f = jnp.where(ok, lo, tau_f)
        return tau_f, count(lambda s, idx: s > tau_f), count(lambda s, idx: s >= tau_f)

    tau, n_gt, n_ge = lax.cond(_any(~ok), bit_search, lambda: (lo, n_gt, n_ge))
    need = kf - n_gt

    def tie_search():
        def body(b, p):
            cand = p + lax.shift_left(jnp.int32(1), jnp.int32(idx_bits - 1) - b)
            n = count(lambda s, idx: (s == tau) & (idx <= cand))
            return jnp.where(n < need, cand, p)
        p = lax.fori_loop(0, idx_bits, body, jnp.full(lo.shape, -1, I32))
        return jnp.where(need > 0.0, p + 1, -1)

    jlim = lax.cond(_any(n_ge > kf), tie_search, lambda: jnp.full(lo.shape, 2 ** idx_bits, I32))
    return tau, jlim


def _selected(score, kidx, tau, jlim):
    return (score > tau) | ((score == tau) & (kidx <= jlim))


def _canonical(score):
    return jnp.where(score == 0.0, 0.0, score)


def _sublane_fold(x, op):
    return op(x.reshape(x.shape[0] // 8, 8, x.shape[1]), axis=0)


def _pattn_kernel(q_ref, qi_ref, wi_ref, ki_ref, k_ref, vt_ref, o_ref, sc_ref, acc_ref,
                  *, tq, topk, n_kv, group, idx_bits):
    i = pl.program_id(1)
    n_chunks = i + 1
    krow = lax.broadcasted_iota(I32, (tq, tq), 0)
    qcol = lax.broadcasted_iota(I32, (tq, tq), 1)

    def index_chunk(c, carry):
        mn, mx, na = carry
        start = pl.multiple_of(c * tq, tq)
        kic = ki_ref[pl.ds(start, tq), :]
        acc = jnp.zeros((tq, tq), F32)
        for h in range(N_IDX_HEADS):
            x = lax.dot_general(kic, qi_ref[h], NT_DIMS, preferred_element_type=F32)
            acc = acc + wi_ref[0, h:h + 1, :] * jnp.maximum(x, 0.0)
        acc = _canonical(acc)
        future = (c == i) & (krow > qcol)
        sc = jnp.where(future, -jnp.inf, acc)
        sc_ref[c] = sc
        finite = sc > -jnp.inf
        mn = jnp.minimum(mn, _sublane_fold(jnp.where(finite, sc, jnp.inf), jnp.min))
        mx = jnp.maximum(mx, _sublane_fold(sc, jnp.max))
        na = na + _sublane_fold(jnp.where(finite, 1.0, 0.0), jnp.sum)
        return mn, mx, na

    mn, mx, na = lax.fori_loop(0, n_chunks, index_chunk,
                               (jnp.full((8, tq), jnp.inf, F32), jnp.full((8, tq), -jnp.inf, F32),
                                jnp.zeros((8, tq), F32)))
    smin = jnp.min(mn, axis=0, keepdims=True)
    smax = jnp.max(mx, axis=0, keepdims=True)
    n_adm = jnp.sum(na, axis=0, keepdims=True)
    total = (n_chunks * tq).astype(F32)

    def count(pred):
        def body(c, cnt):
            hit = pred(sc_ref[c], c * tq + krow)
            return cnt + _sublane_fold(jnp.where(hit, 1.0, 0.0), jnp.sum)
        cnt = lax.fori_loop(0, n_chunks, body, jnp.zeros((8, tq), F32))
        return jnp.sum(cnt, axis=0, keepdims=True)

    tau, jlim = _topk_cut(count, smin, smax, n_adm, total, topk, idx_bits)

    acc_ref[...] = jnp.zeros(acc_ref.shape, F32)
    gw = group * tq

    def attend_chunk(c, carry):
        start = pl.multiple_of(c * tq, tq)
        kidx = c * tq + krow
        sel = _selected(sc_ref[c], kidx, tau, jlim) & (kidx <= i * tq + qcol)
        bias = jnp.where(sel, 0.0, NEG_BIAS)
        bias = jnp.concatenate([bias] * group, axis=1)
        new = []
        for g in range(n_kv):
            m_old, l_old = carry[g]
            kg = k_ref[g, pl.ds(start, tq), :]
            qg = q_ref[g * group:(g + 1) * group].reshape(gw, HEAD_DIM)
            s = lax.dot_general(kg, qg, NT_DIMS, preferred_element_type=F32) + bias
            m_new = jnp.maximum(m_old, jnp.max(_sublane_fold(s, jnp.max), axis=0, keepdims=True))
            p = jnp.exp(s - m_new)
            alpha = jnp.exp(m_old - m_new)
            l_new = alpha * l_old + jnp.sum(_sublane_fold(p, jnp.sum), axis=0, keepdims=True)
            pv = jnp.dot(vt_ref[c, g * HEAD_DIM:(g + 1) * HEAD_DIM, :], p.astype(BF16),
                         preferred_element_type=F32)
            acc_ref[g] = alpha * acc_ref[g] + pv
            new.append((m_new, l_new))
        return tuple(new)

    init = tuple((jnp.full((1, gw), NEG_BIAS, F32), jnp.zeros((1, gw), F32)) for _ in range(n_kv))
    fin = lax.fori_loop(0, n_chunks, attend_chunk, init)

    outs = []
    for g in range(n_kv):
        og = acc_ref[g] / fin[g][1]
        for r in range(group):
            outs.append(og[:, r * tq:(r + 1) * tq])
    o_ref[0] = jnp.concatenate(outs, axis=0).T.astype(BF16)


def _prompt_attention(q, qi, wi_t, kib, kb, vt, *, bsz, tq, topk, n_kv, group):
    n_heads, tokens, _ = q.shape
    s = tokens // bsz
    nb = s // tq
    hk = n_kv * HEAD_DIM
    hq = n_heads * HEAD_DIM
    return pl.pallas_call(
        functools.partial(_pattn_kernel, tq=tq, topk=topk, n_kv=n_kv, group=group,
                          idx_bits=max(s - 1, 1).bit_length()),
        grid=(bsz, nb),
        in_specs=[pl.BlockSpec((n_heads, tq, HEAD_DIM), lambda bi, i: (0, bi * nb + i, 0)),
                  pl.BlockSpec((N_IDX_HEADS, tq, IDX_DIM), lambda bi, i: (0, bi * nb + i, 0)),
                  pl.BlockSpec((1, N_IDX_HEADS, tq), lambda bi, i: (bi, 0, i)),
                  pl.BlockSpec((s, IDX_DIM), lambda bi, i: (bi, 0)),
                  pl.BlockSpec((n_kv, s, HEAD_DIM), lambda bi, i: (0, bi, 0)),
                  pl.BlockSpec((nb, hk, tq), lambda bi, i: (bi, 0, 0))],
        out_specs=pl.BlockSpec((1, tq, hq), lambda bi, i: (bi, i, 0)),
        out_shape=jax.ShapeDtypeStruct((bsz, s, hq), BF16),
        scratch_shapes=[pltpu.VMEM((s // tq, tq, tq), F32),
                        pltpu.VMEM((n_kv, HEAD_DIM, group * tq), F32)],
        compiler_params=_params("parallel", "arbitrary"),
        name="prompt_attention",
    )(q, qi, wi_t, kib, kb, vt)


def _lane_fold(m, op):
    out = m[:, :LANES]
    for j in range(1, m.shape[1] // LANES):
        out = op(out, m[:, j * LANES:(j + 1) * LANES])
    return out


def _sattn_kernel(pt_ref, qbd_ref, qi_ref, wi_ref, kin_ref, kn_ref, vn_ref, cidx_ref, ck_ref, cv_ref,
                  o_ref, idx_buf, k_buf, v_buf, sc_ref, scn_ref, out_ref, sem,
                  *, t, kc, n_pages, topk, n_kv, group, idx_bits):
    b = pl.program_id(0)
    past = n_pages * PAGE_SIZE
    n_chunks = past // kc
    ppc = kc // PAGE_SIZE
    n_heads = n_kv * group
    rows_a = n_heads * t
    slot = b % 2

    def page_copies(batch, sl, p):
        page = pt_ref[batch, p]
        return (pltpu.make_async_copy(cidx_ref.at[page], idx_buf.at[sl, p], sem.at[sl, 0]),
                pltpu.make_async_copy(ck_ref.at[page], k_buf.at[sl, p], sem.at[sl, 1]),
                pltpu.make_async_copy(cv_ref.at[page], v_buf.at[sl, p], sem.at[sl, 2]))

    def start_batch(batch, sl):
        def body(p, carry):
            for cp in page_copies(batch, sl, p):
                cp.start()
            return carry
        lax.fori_loop(0, n_pages, body, 0)

    @pl.when(b == 0)
    def _():
        start_batch(0, 0)

    @pl.when(b + 1 < pl.num_programs(0))
    def _():
        start_batch(b + 1, 1 - slot)

    def wait_page(p, carry):
        for cp in page_copies(b, slot, p):
            cp.wait()
        return carry

    lax.fori_loop(0, n_pages, wait_page, 0)

    def chunk_of(buf, c):
        return jnp.concatenate([buf[slot, c * ppc + j] for j in range(ppc)], axis=1).astype(BF16)

    qi = qi_ref[0]
    wi = wi_ref[0]

    def head_sum(y):
        out = y[0:t]
        for h in range(1, N_IDX_HEADS):
            out = out + y[h * t:(h + 1) * t]
        return out

    def index_chunk(c, carry):
        mn, mx, na = carry
        x = jnp.dot(qi, chunk_of(idx_buf, c), preferred_element_type=F32)
        sc = _canonical(head_sum(wi * jnp.maximum(x, 0.0)))
        sc_ref[c] = sc
        finite = sc > -jnp.inf
        return (jnp.minimum(mn, _lane_fold(jnp.where(finite, sc, jnp.inf), jnp.minimum)),
                jnp.maximum(mx, _lane_fold(sc, jnp.maximum)),
                na + _lane_fold(jnp.where(finite, 1.0, 0.0), jnp.add))

    mn, mx, na = lax.fori_loop(0, n_chunks, index_chunk,
                               (jnp.full((t, LANES), jnp.inf, F32), jnp.full((t, LANES), -jnp.inf, F32),
                                jnp.zeros((t, LANES), F32)))

    rown = lax.broadcasted_iota(I32, (t, LANES), 0)
    coln = lax.broadcasted_iota(I32, (t, LANES), 1)
    xn = lax.dot_general(qi, kin_ref[0], NT_DIMS, preferred_element_type=F32)
    scn = _canonical(head_sum(wi * jnp.maximum(xn, 0.0)))
    scn = jnp.where(coln <= rown, scn, -jnp.inf)
    scn_ref[...] = scn
    smin = jnp.min(jnp.minimum(mn, jnp.where(scn > -jnp.inf, scn, jnp.inf)), axis=1, keepdims=True)
    smax = jnp.max(jnp.maximum(mx, scn), axis=1, keepdims=True)
    n_adm = jnp.sum(na + jnp.where(scn > -jnp.inf, 1.0, 0.0), axis=1, keepdims=True)
    total = float(past + LANES)

    colc = lax.broadcasted_iota(I32, (t, kc), 1)

    def count(pred):
        def body(c, cnt):
            hit = pred(sc_ref[c], c * kc + colc)
            return cnt + _lane_fold(jnp.where(hit, 1.0, 0.0), jnp.add)
        cnt = lax.fori_loop(0, n_chunks, body, jnp.zeros((t, LANES), F32))
        cnt = cnt + jnp.where(pred(scn_ref[...], past + coln), 1.0, 0.0)
        return jnp.sum(cnt, axis=1, keepdims=True)

    tau, jlim = _topk_cut(count, smin, smax, n_adm, total, topk, idx_bits)

    def bias_of(sc, kpos):
        return jnp.where(_selected(sc, kpos, tau, jlim), 0.0, NEG_BIAS)

    qbd = qbd_ref[0]

    def softmax_step(state, s, weigh):
        m_old, l_old, acc = state
        m_new = jnp.maximum(m_old, jnp.max(s, axis=1, keepdims=True))
        p = jnp.exp(s - m_new)
        alpha = jnp.exp(m_old - m_new)
        l_new = alpha * l_old + jnp.sum(p, axis=1, keepdims=True)
        return m_new, l_new, alpha * acc + weigh(p.astype(BF16))

    def attend_chunk(c, state):
        bias = jnp.concatenate([bias_of(sc_ref[c], c * kc + colc)] * n_heads, axis=0)
        s = jnp.dot(qbd, chunk_of(k_buf, c), preferred_element_type=F32) + bias
        vals_t = chunk_of(v_buf, c)
        return softmax_step(state, s, lambda p: lax.dot_general(p, vals_t, NT_DIMS, preferred_element_type=F32))

    hk = n_kv * HEAD_DIM
    state = (jnp.full((rows_a, 1), NEG_BIAS, F32), jnp.zeros((rows_a, 1), F32), jnp.zeros((rows_a, hk), F32))
    state = lax.fori_loop(0, n_chunks, attend_chunk, state)
    biasn = jnp.where(coln <= rown, bias_of(scn_ref[...], past + coln), NEG_BIAS)
    sn = lax.dot_general(qbd, kn_ref[0], NT_DIMS, preferred_element_type=F32)
    sn = sn + jnp.concatenate([biasn] * n_heads, axis=0)
    _, l_fin, acc = softmax_step(state, sn, lambda p: jnp.dot(p, vn_ref[0], preferred_element_type=F32))
    o = acc / l_fin
    for h in range(n_heads):
        g = h // group
        out_ref[:, h * HEAD_DIM:(h + 1) * HEAD_DIM] = o[h * t:(h + 1) * t, g * HEAD_DIM:(g + 1) * HEAD_DIM]
    o_ref[0] = out_ref[...]


def _sample_attention(page_table, qbd, qi, wi, ki_new, k_new, v_new, cache_idx, cache_k, cache_v,
                      *, t, kc, topk, n_kv, group):
    db, n_pages = page_table.shape
    past = n_pages * PAGE_SIZE
    hk = n_kv * HEAD_DIM
    hq = n_kv * group * HEAD_DIM
    blk = lambda b, pt: (b, 0, 0)
    any_spec = pl.BlockSpec(memory_space=pl.ANY)
    grid_spec = pltpu.PrefetchScalarGridSpec(
        num_scalar_prefetch=1,
        grid=(db,),
        in_specs=[pl.BlockSpec((1,) + qbd.shape[1:], blk), pl.BlockSpec((1,) + qi.shape[1:], blk),
                  pl.BlockSpec((1,) + wi.shape[1:], blk), pl.BlockSpec((1,) + ki_new.shape[1:], blk),
                  pl.BlockSpec((1,) + k_new.shape[1:], blk), pl.BlockSpec((1,) + v_new.shape[1:], blk),
                  any_spec, any_spec, any_spec],
        out_specs=pl.BlockSpec((1, t, hq), blk),
        scratch_shapes=[pltpu.VMEM((2, n_pages, IDX_DIM, PAGE_SIZE), F32), pltpu.VMEM((2, n_pages, hk, PAGE_SIZE), F32),
                        pltpu.VMEM((2, n_pages, hk, PAGE_SIZE), F32),
                        pltpu.VMEM((past // kc, t, kc), F32), pltpu.VMEM((t, LANES), F32),
                        pltpu.VMEM((t, hq), F32), pltpu.SemaphoreType.DMA((2, 3))],
    )
    return pl.pallas_call(
        functools.partial(_sattn_kernel, t=t, kc=kc, n_pages=n_pages, topk=topk, n_kv=n_kv, group=group,
                          idx_bits=max(past + LANES - 1, 1).bit_length()),
        grid_spec=grid_spec,
        out_shape=jax.ShapeDtypeStruct((db, t, hq), F32),
        compiler_params=_params("arbitrary"),
        name="sample_attention",
    )(page_table, qbd, qi, wi, ki_new, k_new, v_new, cache_idx, cache_k, cache_v)


def _merge_ffn_kernel(x_ref, conv_ref, attn_ref, wo_ref, g_ref, wg_ref, wu_ref, wd_ref, o_ref, *, c_conv):
    conv = conv_ref[...].astype(BF16)
    attn = attn_ref[...].astype(BF16)
    x = (x_ref[...] + jnp.dot(conv, wo_ref[:c_conv, :], preferred_element_type=F32)
         + jnp.dot(attn, wo_ref[c_conv:, :], preferred_element_type=F32))
    ms = jnp.mean(x * x, axis=-1, keepdims=True)
    h = ((x * lax.rsqrt(ms + 1e-6)) * g_ref[...]).astype(BF16)
    ffn = None
    for ci in range(wg_ref.shape[0]):
        gate = jnp.dot(h, wg_ref[ci], preferred_element_type=F32)
        up = jnp.dot(h, wu_ref[ci], preferred_element_type=F32)
        f = ((gate * jax.nn.sigmoid(gate)) * up).astype(BF16)
        part = jnp.dot(f, wd_ref[ci], preferred_element_type=F32)
        ffn = part if ffn is None else ffn + part
    o_ref[...] = x + ffn


def _merge_ffn(x2d, conv, attn, wo, g, wg, wu, wd, *, tm):
    t, d = x2d.shape
    c_conv = conv.shape[1]
    row = lambda i: (i, 0)
    return pl.pallas_call(
        functools.partial(_merge_ffn_kernel, c_conv=c_conv),
        grid=(t // tm,),
        in_specs=[pl.BlockSpec((tm, d), row), pl.BlockSpec((tm, c_conv), row), pl.BlockSpec((tm, attn.shape[1]), row),
                  _const_spec(wo.shape), _const_spec(g.shape), _const_spec(wg.shape), _const_spec(wu.shape),
                  _const_spec(wd.shape)],
        out_specs=pl.BlockSpec((tm, d), row),
        out_shape=jax.ShapeDtypeStruct((t, d), F32),
        compiler_params=_params("parallel"),
        name="merge_ffn",
    )(x2d, conv, attn, wo, g, wg, wu, wd)


def _ffn_chunks(d_ff):
    for n in (2, 3, 4):
        if d_ff % (n * LANES) == 0:
            return n
    return 1


def kernel(x_prompt, x_sample, cache_k, cache_v, cache_idx_k, state_conv, page_table, attn_norm_g, w_in,
           q_norm_g, k_norm_g, conv_dw_w, conv_dw_b, conv_ln_g, conv_ln_b, w_out, ffn_norm_g, w_gate, w_up,
           w_down):
    bsz, s, d = x_prompt.shape
    db, t, _ = x_sample.shape
    depth = w_in.shape[0]
    assert depth == 1, "single-layer step"
    n_pool = cache_k.shape[1]
    n_kv = cache_k.shape[3]
    c_conv = state_conv.shape[3]
    hk = n_kv * HEAD_DIM
    hq = w_out.shape[1] - c_conv
    group = hq // hk
    hi = N_IDX_HEADS * IDX_DIM
    d_main = 2 * c_conv + hq + 2 * hk + hi
    d_ff = w_gate.shape[2]
    past = page_table.shape[1] * PAGE_SIZE
    assert w_in.shape[2] == d_main + IDX_DIM + N_IDX_HEADS
    assert c_conv % LANES == 0 and hq % LANES == 0 and hk % LANES == 0
    topk_p = min(TOPK_MAX, s // 4)
    topk_s = min(TOPK_MAX, (past + t) // 4)

    tm = min(512, s)
    tq = min(256, s)
    assert s % tm == 0 and s % tq == 0 and (db * t) % min(tm, db * t) == 0 and tq >= topk_p

    w_tail = jnp.pad(w_in[0, :, d_main:], ((0, 0), (0, LANES - IDX_DIM - N_IDX_HEADS)))
    w_packed = jnp.concatenate([w_in[0, :, :d_main], w_tail], axis=1).astype(BF16)
    qg = jnp.tile(q_norm_g, (1, hq // HEAD_DIM))
    kg = jnp.tile(k_norm_g, (1, hk // HEAD_DIM))
    head_of = jnp.arange(hq) // HEAD_DIM
    bd = (head_of[:, None] == head_of[None, :]).astype(BF16)
    wo = w_out[0].astype(BF16)
    n_fc = _ffn_chunks(d_ff)
    col_chunks = lambda w: w.reshape(d, n_fc, d_ff // n_fc).transpose(1, 0, 2).astype(BF16)
    wg, wu = col_chunks(w_gate[0]), col_chunks(w_up[0])
    wd = w_down[0].reshape(n_fc, d_ff // n_fc, d).astype(BF16)
    dims = dict(c_conv=c_conv, hq=hq, hk=hk, hi=hi)

    xp2 = x_prompt.reshape(bsz * s, d)
    tabs_p = _rope_tables(jnp.arange(s, dtype=I32))
    u, q, k, kb, v, vt, qi, ki, kib, wi = _project(xp2, tabs_p, attn_norm_g, w_packed, qg, kg, bd, tm=tm, vc=tq,
                                                   **dims)
    u3 = u.reshape(bsz, s, c_conv)
    conv_p = _conv_prompt(u3, conv_dw_w[0], conv_dw_b, conv_ln_g, conv_ln_b, ts=min(256, s), rt=64)
    n_heads = hq // HEAD_DIM
    attn_p = _prompt_attention(q, qi, wi.reshape(bsz, s, N_IDX_HEADS).transpose(0, 2, 1), kib, kb, vt,
                               bsz=bsz, tq=tq, topk=topk_p, n_kv=n_kv, group=group)
    y_p = _merge_ffn(xp2, conv_p.reshape(bsz * s, c_conv), attn_p.reshape(bsz * s, hq), wo, ffn_norm_g,
                     wg, wu, wd, tm=tm)

    ts_tok = db * t
    tms = min(tm, ts_tok)
    xs2 = x_sample.reshape(ts_tok, d)
    pos_s = past + (jnp.arange(tms, dtype=I32) % t)
    tabs_s = _rope_tables(pos_s)
    us, qs, ks, _, vs, _, qis, kis, kisb, wis = _project(xs2, tabs_s, attn_norm_g, w_packed, qg, kg, bd,
                                                         tm=tms, vc=min(tq, tms), **dims)
    full_s = jnp.concatenate([state_conv[0], us.reshape(db, t, c_conv)], axis=1)
    conv_s = _conv_sample(full_s, conv_dw_w[0], conv_dw_b, conv_ln_g, conv_ln_b, bb=min(8, db), t=t)

    q_ht = qs.reshape(n_heads, db, t, HEAD_DIM).transpose(1, 0, 2, 3)
    kv_of = jnp.arange(n_heads) // group
    onehot = (kv_of[:, None] == jnp.arange(n_kv)[None, :]).astype(BF16)
    qbd = (q_ht[:, :, :, None, :] * onehot[None, :, None, :, None]).reshape(db, n_heads * t, hk)
    qi_ht = qis.reshape(N_IDX_HEADS, db, t, IDX_DIM).transpose(1, 0, 2, 3).reshape(db, N_IDX_HEADS * t, IDX_DIM)
    wi_ht = wis.reshape(db, t, N_IDX_HEADS).transpose(0, 2, 1).reshape(db, N_IDX_HEADS * t, 1)
    pad_rows = lambda a: jnp.pad(a.reshape(db, t, a.shape[-1]), ((0, 0), (0, LANES - t), (0, 0)))
    ksb, vsb = ks.astype(BF16), vs.astype(BF16)
    page_t = lambda c: jnp.moveaxis(c[0], 1, -1).reshape(n_pool, -1, PAGE_SIZE)
    attn_s = _sample_attention(page_table, qbd, qi_ht, wi_ht, pad_rows(kisb), pad_rows(ksb), pad_rows(vsb),
                               page_t(cache_idx_k), page_t(cache_k), page_t(cache_v),
                               t=t, kc=min(1024, past), topk=topk_s, n_kv=n_kv, group=group)
    y_s = _merge_ffn(xs2, conv_s.reshape(ts_tok, c_conv), attn_s.reshape(ts_tok, hq), wo, ffn_norm_g,
                     wg, wu, wd, tm=tms)

    return (y_p.reshape(bsz, s, d), y_s.reshape(db, t, d),
            k.reshape(1, bsz, s, n_kv, HEAD_DIM), v.reshape(1, bsz, s, n_kv, HEAD_DIM),
            ki.reshape(1, bsz, s, IDX_DIM), u3[:, s - (CONV_WIDTH - 1):][None],
            ks.reshape(1, db, t, n_kv, HEAD_DIM), vs.reshape(1, db, t, n_kv, HEAD_DIM),
            kis.reshape(1, db, t, IDX_DIM), full_s[:, t:][None])
```
